```python
import jax, jax.numpy as jnp
from jax import lax
import numpy as np

D_MODEL = 1024
BATCH = 8
SEQ = 4096
DEPTH = 2

CTX_LEN = 256
GRID_W = 64
HEAD_DIM = 64
NA_HEADS = 6
NA_KR = 8
NA_KW = 16
NA_QB = 16
NA_SPAN = NA_QB + NA_KW
SWA_HEADS = 6
SWA_KV_HEADS = 2
SWA_WINDOW = 128
SWA_BLOCK = 128
FNET_GROUPS = 4
FNET_GROUP_DIM = 64
NA_WIDTH = NA_HEADS * HEAD_DIM
SWA_WIDTH = SWA_HEADS * HEAD_DIM
SWA_KV_WIDTH = SWA_KV_HEADS * HEAD_DIM
FNET_WIDTH = FNET_GROUPS * FNET_GROUP_DIM
D_MIX = NA_WIDTH + SWA_WIDTH + FNET_WIDTH
D_IN = 3 * NA_WIDTH + SWA_WIDTH + 2 * SWA_KV_WIDTH + FNET_WIDTH
IN_SPLITS = (NA_WIDTH, 2 * NA_WIDTH, 3 * NA_WIDTH, 3 * NA_WIDTH + SWA_WIDTH,
             3 * NA_WIDTH + SWA_WIDTH + SWA_KV_WIDTH, 3 * NA_WIDTH + SWA_WIDTH + 2 * SWA_KV_WIDTH)
D_FF = 256 * ((8 * D_MODEL // 3 + 255) // 256)
N_SUB = 3
N_MOD = 3 * N_SUB
MACARON_WEIGHT = 0.5
ROPE_BASE = 10000.0
RMS_EPS = 1e-6
NEG_INF = -1e30

kernel_name = 'hybrid_natten_swa_fnet_macaron_dit'


def rms_norm(x, g):
    x32 = x.astype(jnp.float32)
    y = x32 * lax.rsqrt(jnp.mean(x32 * x32, axis=-1, keepdims=True) + RMS_EPS)
    return (y * g.astype(jnp.float32)).astype(x.dtype)


def modulate(h, shift, scale):
    return h * (1 + scale) + shift


def swiglu(h, w1, w2):
    gate, up = jnp.split(h @ w1, 2, axis=-1)
    return (jax.nn.silu(gate) * up) @ w2


def rope_half(x, pos):
    half = x.shape[-1] // 2
    inv = ROPE_BASE ** (-jnp.arange(half, dtype=jnp.float32) / half)
    ang = pos[:, None] * inv[None, :]
    cos = jnp.cos(ang)[:, None, :]
    sin = jnp.sin(ang)[:, None, :]
    x1, x2 = x[..., :half], x[..., half:]
    return jnp.concatenate([x1 * cos - x2 * sin, x1 * sin + x2 * cos], axis=-1)


def axial_rope(x):
    s = x.shape[1]
    t = jnp.arange(s)
    rows = (t // GRID_W).astype(jnp.float32)
    cols = (t % GRID_W).astype(jnp.float32)
    x32 = x.astype(jnp.float32)
    a = x.shape[-1] // 2
    out = jnp.concatenate([rope_half(x32[..., :a], rows), rope_half(x32[..., a:], cols)], axis=-1)
    return out.astype(x.dtype)


def _na_column_tables():
    n_jb = GRID_W // NA_QB
    k_start = np.clip(np.arange(n_jb) * NA_QB - NA_KW // 2, 0, GRID_W - NA_SPAN)
    col_idx = k_start[:, None] + np.arange(NA_SPAN)[None, :]
    q_col = np.arange(n_jb)[:, None] * NA_QB + np.arange(NA_QB)[None, :]
    w_start = np.clip(q_col - NA_KW // 2, 0, GRID_W - NA_KW)[..., None]
    k_col = col_idx[:, None, :]
    valid = (k_col >= w_start) & (k_col < w_start + NA_KW)
    offset = np.clip(k_col - q_col[..., None] + NA_KW - 1, 0, 2 * NA_KW - 2)
    return col_idx, valid, offset


def neighbourhood_attention(q, k, v, kc, vc, rpb):
    b, s, h, hd = q.shape
    rows = s // GRID_W
    kr = min(NA_KR, rows)
    n_jb = GRID_W // NA_QB
    n_lat = kr * NA_SPAN
    scale = hd ** -0.5
    col_idx, col_valid, col_off = _na_column_tables()
    qg = q.reshape(b, rows, n_jb, NA_QB, h, hd)
    kg = k.reshape(b, rows, GRID_W, h, hd)[:, :, col_idx]
    vg = v.reshape(b, rows, GRID_W, h, hd)[:, :, col_idx]
    rpb_col = rpb[:, :, col_off]
    valid = jnp.asarray(col_valid)[None, None, :, :, None, :]

    def row_step(r):
        rs = jnp.clip(r - kr // 2, 0, rows - kr)
        q_r = lax.dynamic_index_in_dim(qg, r, axis=1, keepdims=False)
        k_r = lax.dynamic_slice_in_dim(kg, rs, kr, axis=1)
        v_r = lax.dynamic_slice_in_dim(vg, rs, kr, axis=1)
        row_off = rs + jnp.arange(kr) - r + NA_KR - 1
        bias = jnp.transpose(rpb_col[:, row_off], (0, 2, 3, 1, 4)).astype(jnp.float32)
        s_lat = jnp.einsum('bjqhd,bijshd->bhjqis', q_r, k_r).astype(jnp.float32) * scale + bias
        s_lat = jnp.where(valid, s_lat, NEG_INF).reshape(b, h, n_jb, NA_QB, n_lat)
        s_ctx = jnp.einsum('bjqhd,blhd->bhjql', q_r, kc).astype(jnp.float32) * scale
        p = jax.nn.softmax(jnp.concatenate([s_lat, s_ctx], axis=-1), axis=-1).astype(v.dtype)
        p_lat = p[..., :n_lat].reshape(b, h, n_jb, NA_QB, kr, NA_SPAN)
        p_ctx = p[..., n_lat:]
        return (jnp.einsum('bhjqis,bijshd->bjqhd', p_lat, v_r)
                + jnp.einsum('bhjql,blhd->bjqhd', p_ctx, vc))

    out = lax.map(row_step, jnp.arange(rows))
    return jnp.moveaxis(out, 0, 1).reshape(b, s, h * hd)


def window_gqa_attention(q, k, v, kc, vc, sink):
    b, s, h, hd = q.shape
    kvh = k.shape[2]
    g = h // kvh
    n_ctx = kc.shape[1]
    nb = s // SWA_BLOCK
    span = 3 * SWA_BLOCK
    scale = hd ** -0.5
    qb = q.reshape(b, nb, SWA_BLOCK, kvh, g, hd)
    pad = ((0, 0), (SWA_BLOCK, SWA_BLOCK), (0, 0), (0, 0))
    kp = jnp.pad(k, pad)
    vp = jnp.pad(v, pad)
    rel = jnp.arange(span)[None, :] - SWA_BLOCK - jnp.arange(SWA_BLOCK)[:, None]
    in_window = jnp.abs(rel) <= SWA_WINDOW
    sink_l = jnp.broadcast_to(sink.astype(jnp.float32).reshape(1, kvh, g, 1, 1), (b, kvh, g, SWA_BLOCK, 1))

    def block_step(n):
        q_n = lax.dynamic_index_in_dim(qb, n, axis=1, keepdims=False)
        k_n = lax.dynamic_slice_in_dim(kp, n * SWA_BLOCK, span, axis=1)
        v_n = lax.dynamic_slice_in_dim(vp, n * SWA_BLOCK, span, axis=1)
        kpos = (n - 1) * SWA_BLOCK + jnp.arange(span)
        valid = in_window & ((kpos >= 0) & (kpos < s))[None, :]
        s_lat = jnp.einsum('bqkgd,bskd->bkgqs', q_n, k_n).astype(jnp.float32) * scale
        s_lat = jnp.where(valid, s_lat, NEG_INF)
        s_ctx = jnp.einsum('bqkgd,blkd->bkgql', q_n, kc).astype(jnp.float32) * scale
        p = jax.nn.softmax(jnp.concatenate([s_lat, s_ctx, sink_l], axis=-1), axis=-1).astype(v.dtype)
        return (jnp.einsum('bkgqs,bskd->bqkgd', p[..., :span], v_n)
                + jnp.einsum('bkgql,blkd->bqkgd', p[..., span:span + n_ctx], vc))

    out = lax.map(block_step, jnp.arange(nb))
    return jnp.moveaxis(out, 0, 1).reshape(b, s, h * hd)


def context_attention(qc, kc, vc, sink=None):
    b, l, h, hd = qc.shape
    kvh = kc.shape[2]
    g = h // kvh
    q5 = qc.reshape(b, l, kvh, g, hd)
    sc = jnp.einsum('blkgd,bmkd->bkglm', q5, kc).astype(jnp.float32) * hd ** -0.5
    if sink is not None:
        sk = jnp.broadcast_to(sink.astype(jnp.float32).reshape(1, kvh, g, 1, 1), (b, kvh, g, l, 1))
        sc = jnp.concatenate([sc, sk], axis=-1)
    p = jax.nn.softmax(sc, axis=-1)[..., :l].astype(vc.dtype)
    return jnp.einsum('bkglm,bmkd->blkgd', p, vc).reshape(b, l, h * hd)


def fourier_mix(u):
    b, n, _ = u.shape
    u4 = u.astype(jnp.float32).reshape(b, n, FNET_GROUPS, FNET_GROUP_DIM)
    f = jnp.fft.fft2(u4, axes=(1, 3), norm='ortho').real
    return f.reshape(b, n, FNET_WIDTH).astype(u.dtype)


def project_groups(h, w_in):
    b, n, _ = h.shape
    aq, ak, av, bq, bk, bv, fu = jnp.split(h @ w_in, IN_SPLITS, axis=-1)
    heads = lambda t, nh: t.reshape(b, n, nh, HEAD_DIM)
    return (heads(aq, NA_HEADS), heads(ak, NA_HEADS), heads(av, NA_HEADS),
            heads(bq, SWA_HEADS), heads(bk, SWA_KV_HEADS), heads(bv, SWA_KV_HEADS), fu)


def mods_of(m, sub):
    return m[..., 3 * sub, :], m[..., 3 * sub + 1, :], m[..., 3 * sub + 2, :]


def ffn_sublayer(h, m, sub, g_pre, g_post, w1, w2):
    shift, scale, gate = mods_of(m, sub)
    y = swiglu(modulate(rms_norm(h, g_pre[sub]), shift, scale), w1, w2)
    return h + MACARON_WEIGHT * gate * rms_norm(y, g_post[sub])


def hybrid_layer(x, xc, mod, mod_c, g_pre, g_post, w_ffn_in, w_ffn_out, w_in, w_out, na_rpb, swa_sink, ctx_out):
    x = ffn_sublayer(x, mod, 0, g_pre, g_post, w_ffn_in[0], w_ffn_out[0])
    xc = ffn_sublayer(xc, mod_c, 0, g_pre, g_post, w_ffn_in[0], w_ffn_out[0])
    sh, sc, gt = mods_of(mod, 1)
    shc, scc, gtc = mods_of(mod_c, 1)
    h = modulate(rms_norm(x, g_pre[1]), sh, sc)
    hc = modulate(rms_norm(xc, g_pre[1]), shc, scc)
    aq, ak, av, bq, bk, bv, fu = project_groups(h, w_in)
    aqc, akc, avc, bqc, bkc, bvc, fuc = project_groups(hc, w_in)
    bq = axial_rope(bq)
    bk = axial_rope(bk)
    o_a = neighbourhood_attention(aq, ak, av, akc, avc, na_rpb)
    o_b = window_gqa_attention(bq, bk, bv, bkc, bvc, swa_sink)
    o_c = fourier_mix(fu)
    o = jnp.concatenate([o_a, o_b, o_c], axis=-1) @ w_out
    x = x + gt * rms_norm(o, g_post[1])
    x = ffn_sublayer(x, mod, 2, g_pre, g_post, w_ffn_in[1], w_ffn_out[1])
    if ctx_out:
        oc = jnp.concatenate([context_attention(aqc, akc, avc),
                              context_attention(bqc, bkc, bvc, swa_sink),
                              fourier_mix(fuc)], axis=-1) @ w_out
        xc = xc + gtc * rms_norm(oc, g_post[1])
        xc = ffn_sublayer(xc, mod_c, 2, g_pre, g_post, w_ffn_in[1], w_ffn_out[1])
    else:
        xc = None
    return x, xc


def setup_inputs(seed: int = 0) -> dict:
    key = jax.random.key(seed)
    ks = jax.random.split(key, 14)
    nrm = jax.random.normal
    f32 = jnp.float32
    return {
        'x': nrm(ks[0], (BATCH, SEQ, D_MODEL), f32),
        'c': nrm(ks[1], (BATCH, D_MODEL), f32),
        'ctx': nrm(ks[2], (BATCH, CTX_LEN, D_MODEL), f32),
        'c_ctx': nrm(ks[3], (D_MODEL,), f32),
        'w_mod': nrm(ks[4], (DEPTH, D_MODEL, N_MOD * D_MODEL), f32) * (0.5 * D_MODEL ** -0.5),
        'b_mod': nrm(ks[5], (DEPTH, N_MOD * D_MODEL), f32) * 0.01,
        'g_pre': 1.0 + 0.01 * nrm(ks[6], (DEPTH, N_SUB, D_MODEL), f32),
        'g_post': 1.0 + 0.01 * nrm(ks[7], (DEPTH, N_SUB, D_MODEL), f32),
        'w_ffn_in': nrm(ks[8], (DEPTH, 2, D_MODEL, 2 * D_FF), f32) * D_MODEL ** -0.5,
        'w_ffn_out': nrm(ks[9], (DEPTH, 2, D_FF, D_MODEL), f32) * D_FF ** -0.5,
        'w_in': nrm(ks[10], (DEPTH, D_MODEL, D_IN), f32) * D_MODEL ** -0.5,
        'w_out': nrm(ks[11], (DEPTH, D_MIX, D_MODEL), f32) * D_MIX ** -0.5,
        'na_rpb': nrm(ks[12], (DEPTH, NA_HEADS, 2 * NA_KR - 1, 2 * NA_KW - 1), f32) * 0.1,
        'swa_sink': nrm(ks[13], (DEPTH, SWA_HEADS), f32) * 0.5,
    }


def reference(x, c, ctx, c_ctx, w_mod, b_mod, g_pre, g_post, w_ffn_in, w_ffn_out, w_in, w_out, na_rpb, swa_sink):
    b = x.shape[0]
    xc = ctx
    c_act = jax.nn.silu(c)
    cc_act = jax.nn.silu(c_ctx)
    for layer in range(DEPTH):
        mod = (c_act @ w_mod[layer] + b_mod[layer]).reshape(b, 1, N_MOD, D_MODEL)
        mod_c = (cc_act @ w_mod[layer] + b_mod[layer]).reshape(N_MOD, D_MODEL)
        x, xc = hybrid_layer(x, xc, mod, mod_c, g_pre[layer], g_post[layer], w_ffn_in[layer], w_ffn_out[layer],
                             w_in[layer], w_out[layer], na_rpb[layer], swa_sink[layer],
                             ctx_out=(layer < DEPTH - 1))
    return x
```

```python
import functools

import numpy as np
import jax
import jax.numpy as jnp
from jax import lax
from jax.experimental import pallas as pl
from jax.experimental.pallas import tpu as pltpu

F32 = jnp.float32
BF16 = jnp.bfloat16

D_MODEL = 1024
GRID_W = 64
HEAD_DIM = 64
NA_HEADS = 6
NA_KR = 8
NA_KW = 16
SWA_HEADS = 6
SWA_KV_HEADS = 2
SWA_WINDOW = 128
SWA_BLOCK = 128
FNET_GROUPS = 4
FNET_GROUP_DIM = 64
NA_WIDTH = NA_HEADS * HEAD_DIM
SWA_WIDTH = SWA_HEADS * HEAD_DIM
SWA_KV_WIDTH = SWA_KV_HEADS * HEAD_DIM
FNET_WIDTH = FNET_GROUPS * FNET_GROUP_DIM
D_IN = 3 * NA_WIDTH + SWA_WIDTH + 2 * SWA_KV_WIDTH + FNET_WIDTH
D_FF = 256 * ((8 * D_MODEL // 3 + 255) // 256)
N_SUB = 3
N_MOD = 3 * N_SUB
MACARON_WEIGHT = 0.5
ROPE_BASE = 10000.0
RMS_EPS = 1e-6
NEG_INF = -1e30
QK_SCALE = HEAD_DIM ** -0.5

V7X_LANES = 128
V7X_VMEM_LIMIT_BYTES = 60000 * 1024
MIB = 1024 * 1024

TOKEN_TILE = 512
FFN_CHUNK = 256
MOD_TILE = 1152
NA_ROWS_PER_STEP = 8
NA_PAIR_ROWS = 2
NA_PAIR_KEY_ROWS = 10
FFT_N1 = 16
FFT_N2 = 256


def _params(vmem_bytes, n_grid):
    limit = int(min(max(vmem_bytes, 16 * MIB), V7X_VMEM_LIMIT_BYTES))
    return pltpu.CompilerParams(dimension_semantics=("arbitrary",) * n_grid, vmem_limit_bytes=limit)


def _rms(x, g):
    return x * lax.rsqrt(jnp.mean(x * x, axis=-1, keepdims=True) + RMS_EPS) * g


def _nt_dot(a, b):
    return lax.dot_general(a, b, (((1,), (1,)), ((), ())), preferred_element_type=F32)


def _dot(a, b):
    return jnp.dot(a, b, preferred_element_type=F32)


def _mod_kernel(c_ref, w_ref, b_ref, o_ref):
    a = c_ref[...]
    act = a / (1.0 + jnp.exp(-a))
    o_ref[0] = _dot(act.astype(BF16), w_ref[0].astype(BF16)) + b_ref[0]


def _modulation(c_rows, w_mod, b_mod):
    depth, _, n_out = w_mod.shape
    rows = c_rows.shape[0]
    return pl.pallas_call(
        _mod_kernel,
        grid=(depth, n_out // MOD_TILE),
        in_specs=[
            pl.BlockSpec((rows, D_MODEL), lambda l, j: (0, 0)),
            pl.BlockSpec((1, D_MODEL, MOD_TILE), lambda l, j: (l, 0, j)),
            pl.BlockSpec((1, 1, MOD_TILE), lambda l, j: (l, 0, j)),
        ],
        out_specs=pl.BlockSpec((1, rows, MOD_TILE), lambda l, j: (l, 0, j)),
        out_shape=jax.ShapeDtypeStruct((depth, rows, n_out), F32),
        compiler_params=_params(4 * D_MODEL * MOD_TILE * 4, 2),
        name="modulation",
    )(c_rows, w_mod, b_mod.reshape(depth, 1, n_out))


def _ffn_kernel(x_ref, mod_ref, gpre_ref, gpost_ref, w1_ref, w2_ref, o_ref, act_ref, *, sub):
    x = x_ref[...]
    shift = mod_ref[0, 3 * sub:3 * sub + 1, :]
    scale = mod_ref[0, 3 * sub + 1:3 * sub + 2, :]
    gate = mod_ref[0, 3 * sub + 2:3 * sub + 3, :]
    h = _rms(x, gpre_ref[sub:sub + 1, :]) * (1.0 + scale) + shift
    hb = h.astype(BF16)
    for c in range(D_FF // FFN_CHUNK):
        lo = c * FFN_CHUNK
        g = _dot(hb, w1_ref[:, lo:lo + FFN_CHUNK])
        u = _dot(hb, w1_ref[:, D_FF + lo:D_FF + lo + FFN_CHUNK])
        act_ref[:, lo:lo + FFN_CHUNK] = (g / (1.0 + jnp.exp(-g)) * u).astype(BF16)
    y = _dot(act_ref[...], w2_ref[...])
    o_ref[...] = x + MACARON_WEIGHT * gate * _rms(y, gpost_ref[sub:sub + 1, :])


def _ffn(x, mod, g_pre, g_post, w1, w2, sub, tokens_per_mod_row):
    n_tok = x.shape[0]
    tiles_per_row = tokens_per_mod_row // TOKEN_TILE
    vmem = (2 * (D_MODEL * 2 * D_FF + D_FF * D_MODEL) * 2 + 4 * TOKEN_TILE * D_MODEL * 4
            + TOKEN_TILE * D_FF * 2 + 8 * MIB)
    return pl.pallas_call(
        functools.partial(_ffn_kernel, sub=sub),
        grid=(n_tok // TOKEN_TILE,),
        in_specs=[
            pl.BlockSpec((TOKEN_TILE, D_MODEL), lambda i: (i, 0)),
            pl.BlockSpec((1, N_MOD, D_MODEL), lambda i: (i // tiles_per_row, 0, 0)),
            pl.BlockSpec((N_SUB, D_MODEL), lambda i: (0, 0)),
            pl.BlockSpec((N_SUB, D_MODEL), lambda i: (0, 0)),
            pl.BlockSpec((D_MODEL, 2 * D_FF), lambda i: (0, 0)),
            pl.BlockSpec((D_FF, D_MODEL), lambda i: (0, 0)),
        ],
        out_specs=pl.BlockSpec((TOKEN_TILE, D_MODEL), lambda i: (i, 0)),
        out_shape=jax.ShapeDtypeStruct((n_tok, D_MODEL), F32),
        scratch_shapes=[pltpu.VMEM((TOKEN_TILE, D_FF), BF16)],
        compiler_params=_params(vmem, 1),
        name="ffn",
    )(x, mod, g_pre, g_post, w1, w2)


_KVB_WIDTH = 4 * SWA_KV_WIDTH


def _inproj_kernel(*refs, rope):
    if rope:
        (x_ref, mod_ref, gpre_ref, w_ref, cos_ref, sina_ref, sinb_ref,
         qa_ref, ka_ref, va_ref, qb_ref, kvb_ref, fu_ref) = refs
    else:
        (x_ref, mod_ref, gpre_ref, w_ref,
         qa_ref, ka_ref, va_ref, qb_ref, kvb_ref, fu_ref) = refs
    x = x_ref[...]
    shift = mod_ref[0, 3:4, :]
    scale = mod_ref[0, 4:5, :]
    h = _rms(x, gpre_ref[1:2, :]) * (1.0 + scale) + shift
    r = _dot(h.astype(BF16), w_ref[...])
    o = 0
    qa_ref[...] = (r[:, o:o + NA_WIDTH] * QK_SCALE).astype(BF16)
    o += NA_WIDTH
    ka_ref[...] = r[:, o:o + NA_WIDTH].astype(BF16)
    o += NA_WIDTH
    va_ref[...] = r[:, o:o + NA_WIDTH].astype(BF16)
    o += NA_WIDTH
    qk = r[:, o:o + SWA_WIDTH + SWA_KV_WIDTH]
    o += SWA_WIDTH + SWA_KV_WIDTH
    if rope:
        reps = (SWA_WIDTH + SWA_KV_WIDTH) // V7X_LANES
        width = qk.shape[1]
        cos = jnp.concatenate([cos_ref[...]] * reps, axis=1)
        sina = jnp.concatenate([sina_ref[...]] * reps, axis=1)
        sinb = jnp.concatenate([sinb_ref[...]] * reps, axis=1)
        half = HEAD_DIM // 4
        qk = (qk * cos + pltpu.roll(qk, width - half, axis=1) * sina
              + pltpu.roll(qk, half, axis=1) * sinb)
    qb_ref[...] = (qk[:, :SWA_WIDTH] * QK_SCALE).astype(BF16)
    kb = qk[:, SWA_WIDTH:]
    vb = r[:, o:o + SWA_KV_WIDTH]
    o += SWA_KV_WIDTH
    kvb_ref[:, 0:SWA_KV_WIDTH] = kb.astype(BF16)
    kvb_ref[:, SWA_KV_WIDTH:2 * SWA_KV_WIDTH] = pltpu.roll(kb, HEAD_DIM, axis=1).astype(BF16)
    kvb_ref[:, 2 * SWA_KV_WIDTH:3 * SWA_KV_WIDTH] = vb.astype(BF16)
    kvb_ref[:, 3 * SWA_KV_WIDTH:4 * SWA_KV_WIDTH] = pltpu.roll(vb, HEAD_DIM, axis=1).astype(BF16)
    fu_ref[...] = r[:, o:o + FNET_WIDTH].astype(BF16)


def _rope_tables(seq):
    t = jnp.arange(seq)
    rows = (t // GRID_W).astype(F32)
    cols = (t % GRID_W).astype(F32)
    half = HEAD_DIM // 4
    inv = ROPE_BASE ** (-jnp.arange(half, dtype=F32) / half)
    ang_r = rows[:, None] * inv[None, :]
    ang_c = cols[:, None] * inv[None, :]
    cos_h = jnp.concatenate([jnp.cos(ang_r)] * 2 + [jnp.cos(ang_c)] * 2, axis=1)
    sin_h = jnp.concatenate([jnp.sin(ang_r)] * 2 + [jnp.sin(ang_c)] * 2, axis=1)
    first = (np.arange(HEAD_DIM) % (2 * half)) < half
    sina_h = jnp.where(first[None, :], -sin_h, 0.0)
    sinb_h = jnp.where(first[None, :], 0.0, sin_h)
    two = lambda a: jnp.concatenate([a, a], axis=1)
    return two(cos_h), two(sina_h), two(sinb_h)


def _inproj(x, mod, g_pre, w_in, tokens_per_mod_row, rope_tables):
    n_tok = x.shape[0]
    tiles_per_row = tokens_per_mod_row // TOKEN_TILE
    rope = rope_tables is not None
    in_specs = [
        pl.BlockSpec((TOKEN_TILE, D_MODEL), lambda i: (i, 0)),
        pl.BlockSpec((1, N_MOD, D_MODEL), lambda i: (i // tiles_per_row, 0, 0)),
        pl.BlockSpec((N_SUB, D_MODEL), lambda i: (0, 0)),
        pl.BlockSpec((D_MODEL, D_IN), lambda i: (0, 0)),
    ]
    args = [x, mod, g_pre, w_in]
    if rope:
        for tab in rope_tables:
            in_specs.append(pl.BlockSpec((TOKEN_TILE, V7X_LANES), lambda i: (i % tiles_per_row, 0)))
            args.append(tab)
    widths = (NA_WIDTH, NA_WIDTH, NA_WIDTH, SWA_WIDTH, _KVB_WIDTH, FNET_WIDTH)
    vmem = 2 * D_MODEL * D_IN * 2 + 4 * TOKEN_TILE * D_MODEL * 4 + 6 * TOKEN_TILE * D_IN * 4 + 8 * MIB
    return pl.pallas_call(
        functools.partial(_inproj_kernel, rope=rope),
        grid=(n_tok // TOKEN_TILE,),
        in_specs=in_specs,
        out_specs=[pl.BlockSpec((TOKEN_TILE, w), lambda i: (i, 0)) for w in widths],
        out_shape=[jax.ShapeDtypeStruct((n_tok, w), BF16) for w in widths],
        compiler_params=_params(vmem, 1),
        name="inproj_rope" if rope else "inproj",
    )(*args)


def _half_mask(rows, half):
    lane = lax.broadcasted_iota(jnp.int32, (rows, V7X_LANES), 1)
    return (lane >= HEAD_DIM) if half else (lane < HEAD_DIM)


def _attend(q, keys, values, biases, extra_logit=None):
    scores = []
    for k, b in zip(keys, biases):
        s = _nt_dot(q, k)
        scores.append(s if b is None else s + b)
    m = scores[0].max(axis=-1, keepdims=True)
    for s in scores[1:]:
        m = jnp.maximum(m, s.max(axis=-1, keepdims=True))
    if extra_logit is not None:
        m = jnp.maximum(m, extra_logit)
    denom = None
    out = None
    for s, v in zip(scores, values):
        p = jnp.exp(s - m)
        d = p.sum(axis=-1, keepdims=True)
        denom = d if denom is None else denom + d
        pv = _dot(p.astype(BF16), v)
        out = pv if out is None else out + pv
    if extra_logit is not None:
        denom = denom + jnp.exp(extra_logit - m)
    return out / denom


_NA_PAIRS_PER_STEP = NA_ROWS_PER_STEP // NA_PAIR_ROWS
_NA_Q = NA_PAIR_ROWS * GRID_W
_NA_K = NA_PAIR_KEY_ROWS * GRID_W


def _na_kernel(q_ref, k_ref, v_ref, kc_ref, vc_ref, bias_ref, o_ref, *, grid_rows):
    j = pl.program_id(2)
    kc = kc_ref[...]
    vc = vc_ref[...]
    for t in range(_NA_PAIRS_PER_STEP):
        pair = j * _NA_PAIRS_PER_STEP + t
        start_row = jnp.clip(NA_PAIR_ROWS * pair - NA_KR // 2, 0, grid_rows - NA_PAIR_KEY_ROWS)
        start = pl.multiple_of(start_row * GRID_W, 2 * GRID_W)
        q = q_ref[t * _NA_Q:(t + 1) * _NA_Q, :]
        k = k_ref[pl.ds(start, _NA_K), :]
        v = v_ref[pl.ds(start, _NA_K), :]
        res = []
        for half in range(2):
            qm = jnp.where(_half_mask(_NA_Q, half), q, jnp.zeros_like(q))
            res.append(_attend(qm, (k, kc), (v, vc), (bias_ref[0, half, t], None)))
        o_ref[t * _NA_Q:(t + 1) * _NA_Q, :] = jnp.where(
            _half_mask(_NA_Q, 0), res[0], res[1]).astype(BF16)


def _na_bias_tables(rpb, grid_rows):
    n_blocks = grid_rows // NA_ROWS_PER_STEP
    cases = (0, min(1, n_blocks - 1), n_blocks - 1)
    kr = min(NA_KR, grid_rows)
    ro = np.zeros((3, _NA_PAIRS_PER_STEP, NA_PAIR_ROWS, NA_PAIR_KEY_ROWS), np.int32)
    rvalid = np.zeros(ro.shape, bool)
    for ci, jb in enumerate(cases):
        for t in range(_NA_PAIRS_PER_STEP):
            pair = jb * _NA_PAIRS_PER_STEP + t
            start_row = int(np.clip(NA_PAIR_ROWS * pair - NA_KR // 2, 0, grid_rows - NA_PAIR_KEY_ROWS))
            for l in range(NA_PAIR_ROWS):
                rq = NA_PAIR_ROWS * pair + l
                rs = int(np.clip(rq - kr // 2, 0, grid_rows - kr))
                for i in range(NA_PAIR_KEY_ROWS):
                    rk = start_row + i
                    rvalid[ci, t, l, i] = rs <= rk < rs + kr
                    ro[ci, t, l, i] = int(np.clip(rk - rq + NA_KR - 1, 0, 2 * NA_KR - 2))
    cq = np.arange(GRID_W)[:, None]
    ck = np.arange(GRID_W)[None, :]
    ws = np.clip(cq - NA_KW // 2, 0, GRID_W - NA_KW)
    cvalid = (ck >= ws) & (ck < ws + NA_KW)
    co = np.clip(ck - cq + NA_KW - 1, 0, 2 * NA_KW - 2)
    vals = rpb[:, ro[..., None, None], co[None, None, None, None]]
    valid = rvalid[..., None, None] & cvalid[None, None, None, None]
    vals = jnp.where(valid[None], vals.astype(F32), NEG_INF)
    vals = jnp.transpose(vals, (1, 0, 2, 3, 5, 4, 6))
    return vals.reshape(3, NA_HEADS, _NA_PAIRS_PER_STEP, _NA_Q, _NA_K)


def _na_attention(qa, ka, va, kac, vac, bias_tab, batch, seq, n_ctx):
    grid_rows = seq // GRID_W
    n_blocks = grid_rows // NA_ROWS_PER_STEP
    rows_q = NA_ROWS_PER_STEP * GRID_W
    n_pairs = NA_HEADS // 2

    def case_of(j):
        return jnp.where(j == 0, 0, jnp.where(j == n_blocks - 1, 2, 1))

    bias_block = (1, 2, _NA_PAIRS_PER_STEP, _NA_Q, _NA_K)
    vmem = (2 * 2 * _NA_PAIRS_PER_STEP * _NA_Q * _NA_K * 4 + 4 * seq * V7X_LANES * 2 + 16 * MIB)
    return pl.pallas_call(
        functools.partial(_na_kernel, grid_rows=grid_rows),
        grid=(batch, n_pairs, n_blocks),
        in_specs=[
            pl.BlockSpec((rows_q, V7X_LANES), lambda b, p, j: (b * n_blocks + j, p)),
            pl.BlockSpec((seq, V7X_LANES), lambda b, p, j: (b, p)),
            pl.BlockSpec((seq, V7X_LANES), lambda b, p, j: (b, p)),
            pl.BlockSpec((n_ctx, V7X_LANES), lambda b, p, j: (b, p)),
            pl.BlockSpec((n_ctx, V7X_LANES), lambda b, p, j: (b, p)),
            pl.BlockSpec(bias_block, lambda b, p, j: (case_of(j), p, 0, 0, 0)),
        ],
        out_specs=pl.BlockSpec((rows_q, V7X_LANES), lambda b, p, j: (b * n_blocks + j, p)),
        out_shape=jax.ShapeDtypeStruct((batch * seq, NA_WIDTH), BF16),
        compiler_params=_params(vmem, 3),
        name="na_attention",
    )(qa, ka, va, kac, vac, bias_tab)


_SWA_SPAN = 3 * SWA_BLOCK
_SWA_GROUP = SWA_HEADS // SWA_KV_HEADS


def _swa_kv_slot(head):
    return (head // _SWA_GROUP) ^ (head % 2)


def _swa_kernel(sink_ref, q_ref, kv_ref, kvc_ref, mask_ref, o_ref, *, seq):
    n = pl.program_id(1)
    start = pl.multiple_of(jnp.clip((n - 1) * SWA_BLOCK, 0, seq - _SWA_SPAN), SWA_BLOCK)
    mask = mask_ref[0]
    for blk in range(SWA_HEADS // 2):
        q = q_ref[:, blk * V7X_LANES:(blk + 1) * V7X_LANES]
        res = []
        for half in range(2):
            head = 2 * blk + half
            slot = _swa_kv_slot(head)
            k = kv_ref[pl.ds(start, _SWA_SPAN), slot * V7X_LANES:(slot + 1) * V7X_LANES]
            v = kv_ref[pl.ds(start, _SWA_SPAN), (2 + slot) * V7X_LANES:(3 + slot) * V7X_LANES]
            kc = kvc_ref[:, slot * V7X_LANES:(slot + 1) * V7X_LANES]
            vc = kvc_ref[:, (2 + slot) * V7X_LANES:(3 + slot) * V7X_LANES]
            qm = jnp.where(_half_mask(SWA_BLOCK, half), q, jnp.zeros_like(q))
            sink = jnp.full((SWA_BLOCK, 1), sink_ref[head], F32)
            res.append(_attend(qm, (k, kc), (v, vc), (mask, None), extra_logit=sink))
        o_ref[:, blk * V7X_LANES:(blk + 1) * V7X_LANES] = jnp.where(
            _half_mask(SWA_BLOCK, 0), res[0], res[1]).astype(BF16)


def _swa_masks(seq):
    nb = seq // SWA_BLOCK
    out = np.zeros((3, SWA_BLOCK, _SWA_SPAN), np.float32)
    for ci, n in enumerate((0, min(1, nb - 1), nb - 1)):
        start = int(np.clip((n - 1) * SWA_BLOCK, 0, seq - _SWA_SPAN))
        kpos = start + np.arange(_SWA_SPAN)[None, :]
        qpos = n * SWA_BLOCK + np.arange(SWA_BLOCK)[:, None]
        out[ci] = np.where(np.abs(kpos - qpos) <= SWA_WINDOW, 0.0, NEG_INF)
    return jnp.asarray(out)


def _swa_attention(qb, kvb, kvbc, sink, batch, seq, n_ctx):
    nb = seq // SWA_BLOCK

    def case_of(n):
        return jnp.where(n == 0, 0, jnp.where(n == nb - 1, 2, 1))

    vmem = 4 * seq * _KVB_WIDTH * 2 + 16 * MIB
    return pl.pallas_call(
        functools.partial(_swa_kernel, seq=seq),
        grid=(batch, nb),
        in_specs=[
            pl.BlockSpec(memory_space=pltpu.SMEM),
            pl.BlockSpec((SWA_BLOCK, SWA_WIDTH), lambda b, n: (b * nb + n, 0)),
            pl.BlockSpec((seq, _KVB_WIDTH), lambda b, n: (b, 0)),
            pl.BlockSpec((n_ctx, _KVB_WIDTH), lambda b, n: (b, 0)),
            pl.BlockSpec((1, SWA_BLOCK, _SWA_SPAN), lambda b, n: (case_of(n), 0, 0)),
        ],
        out_specs=pl.BlockSpec((SWA_BLOCK, SWA_WIDTH), lambda b, n: (b * nb + n, 0)),
        out_shape=jax.ShapeDtypeStruct((batch * seq, SWA_WIDTH), BF16),
        compiler_params=_params(vmem, 2),
        name="swa_attention",
    )(sink, qb, kvb, kvbc, _swa_masks(seq))


def _dft_constants():
    jc = np.arange(FNET_GROUP_DIM)
    ang_c = 2.0 * np.pi * ((jc[:, None] * jc[None, :]) % FNET_GROUP_DIM) / FNET_GROUP_DIM
    eye = np.eye(FNET_GROUPS)
    bd_cos = np.kron(eye, np.cos(ang_c))
    bd_sin = np.kron(eye, np.sin(ang_c))
    chan = np.concatenate([bd_cos, -bd_sin], axis=1)
    jp = np.arange(FFT_N2)
    ang_p = 2.0 * np.pi * ((jp[:, None] * jp[None, :]) % FFT_N2) / FFT_N2
    cp, sp = np.cos(ang_p), np.sin(ang_p)
    pos = np.block([[cp, sp], [-sp, cp]])
    return jnp.asarray(chan, F32), jnp.asarray(pos, F32)


def _twiddle_tables(n_total):
    n1 = np.arange(FFT_N1)[:, None]
    k2 = np.arange(FFT_N2)[None, :]
    ang = 2.0 * np.pi * ((n1 * k2) % n_total) / n_total
    bc = lambda a: jnp.asarray(np.repeat(a[:, :, None], V7X_LANES, axis=2), F32)
    return bc(np.cos(ang)), bc(np.sin(ang))


def _dft256_of_channels(u, chan, pos):
    z = _dot(u, chan)
    zs = jnp.concatenate([z[:, :FNET_WIDTH], z[:, FNET_WIDTH:]], axis=0).astype(BF16)
    a = _dot(pos, zs)
    return a[:FFT_N2], a[FFT_N2:]


def _fnet_kernel(u_ref, chan_ref, pos_ref, twc_ref, tws_ref, o_ref, bre_ref, bim_ref, *, scale):
    chan = chan_ref[...].astype(BF16)
    pos = pos_ref[...].astype(BF16)
    for n1 in range(FFT_N1):
        u = u_ref[0, :, n1 * FNET_WIDTH:(n1 + 1) * FNET_WIDTH]
        a_re, a_im = _dft256_of_channels(u, chan, pos)
        if n1 == 0:
            bre_ref[0] = a_re
            bim_ref[0] = a_im
        else:
            c = jnp.concatenate([twc_ref[n1]] * (FNET_WIDTH // V7X_LANES), axis=1)
            s = jnp.concatenate([tws_ref[n1]] * (FNET_WIDTH // V7X_LANES), axis=1)
            bre_ref[n1] = a_re * c + a_im * s
            bim_ref[n1] = a_im * c - a_re * s
    for k1 in range(FFT_N1):
        acc = bre_ref[0]
        for n1 in range(1, FFT_N1):
            m = (n1 * k1) % FFT_N1
            c = float(np.cos(2.0 * np.pi * m / FFT_N1))
            s = float(np.sin(2.0 * np.pi * m / FFT_N1))
            if abs(c) > 1e-9:
                acc = acc + c * bre_ref[n1]
            if abs(s) > 1e-9:
                acc = acc + s * bim_ref[n1]
        o_ref[k1 * FFT_N2:(k1 + 1) * FFT_N2, :] = (acc * scale).astype(BF16)


def _fnet(fu, batch, seq):
    assert seq == FFT_N1 * FFT_N2
    chan, pos = _dft_constants()
    twc, tws = _twiddle_tables(seq)
    scale = float((seq * FNET_GROUP_DIM) ** -0.5)
    u = fu.reshape(batch, FFT_N2, FFT_N1 * FNET_WIDTH)
    vmem = 2 * FFT_N1 * FFT_N2 * FNET_WIDTH * 4 + 8 * seq * FNET_WIDTH * 2 + 16 * MIB
    return pl.pallas_call(
        functools.partial(_fnet_kernel, scale=scale),
        grid=(batch,),
        in_specs=[
            pl.BlockSpec((1, FFT_N2, FFT_N1 * FNET_WIDTH), lambda b: (b, 0, 0)),
            pl.BlockSpec(chan.shape, lambda b: (0, 0)),
            pl.BlockSpec(pos.shape, lambda b: (0, 0)),
            pl.BlockSpec(twc.shape, lambda b: (0, 0, 0)),
            pl.BlockSpec(tws.shape, lambda b: (0, 0, 0)),
        ],
        out_specs=pl.BlockSpec((seq, FNET_WIDTH), lambda b: (b, 0)),
        out_shape=jax.ShapeDtypeStruct((batch * seq, FNET_WIDTH), BF16),
        scratch_shapes=[pltpu.VMEM((FFT_N1, FFT_N2, FNET_WIDTH), F32),
                        pltpu.VMEM((FFT_N1, FFT_N2, FNET_WIDTH), F32)],
        compiler_params=_params(vmem, 1),
        name="fnet",
    )(u, chan, pos, twc, tws)


def _ctx_mixer_kernel(sink_ref, qa_ref, ka_ref, va_ref, qb_ref, kvb_ref, fu_ref, chan_ref, pos_ref,
                      oa_ref, ob_ref, oc_ref, *, n_ctx, scale):
    for blk in range(NA_HEADS // 2):
        sl = slice(blk * V7X_LANES, (blk + 1) * V7X_LANES)
        q, k, v = qa_ref[:, sl], ka_ref[:, sl], va_ref[:, sl]
        res = []
        for half in range(2):
            qm = jnp.where(_half_mask(n_ctx, half), q, jnp.zeros_like(q))
            res.append(_attend(qm, (k,), (v,), (None,)))
        oa_ref[:, sl] = jnp.where(_half_mask(n_ctx, 0), res[0], res[1]).astype(BF16)
    for blk in range(SWA_HEADS // 2):
        sl = slice(blk * V7X_LANES, (blk + 1) * V7X_LANES)
        q = qb_ref[:, sl]
        res = []
        for half in range(2):
            head = 2 * blk + half
            slot = _swa_kv_slot(head)
            k = kvb_ref[:, slot * V7X_LANES:(slot + 1) * V7X_LANES]
            v = kvb_ref[:, (2 + slot) * V7X_LANES:(3 + slot) * V7X_LANES]
            qm = jnp.where(_half_mask(n_ctx, half), q, jnp.zeros_like(q))
            sink = jnp.full((n_ctx, 1), sink_ref[head], F32)
            res.append(_attend(qm, (k,), (v,), (None,), extra_logit=sink))
        ob_ref[:, sl] = jnp.where(_half_mask(n_ctx, 0), res[0], res[1]).astype(BF16)
    a_re, _ = _dft256_of_channels(fu_ref[...], chan_ref[...].astype(BF16), pos_ref[...].astype(BF16))
    oc_ref[...] = (a_re * scale).astype(BF16)


def _ctx_mixer(qa, ka, va, qb, kvb, fu, sink, batch, n_ctx):
    assert n_ctx == FFT_N2
    chan, pos = _dft_constants()
    scale = float((n_ctx * FNET_GROUP_DIM) ** -0.5)
    tok = lambda w: pl.BlockSpec((n_ctx, w), lambda b: (b, 0))
    widths = (NA_WIDTH, SWA_WIDTH, FNET_WIDTH)
    return pl.pallas_call(
        functools.partial(_ctx_mixer_kernel, n_ctx=n_ctx, scale=scale),
        grid=(batch,),
        in_specs=[
            pl.BlockSpec(memory_space=pltpu.SMEM),
            tok(NA_WIDTH), tok(NA_WIDTH), tok(NA_WIDTH), tok(SWA_WIDTH), tok(_KVB_WIDTH), tok(FNET_WIDTH),
            pl.BlockSpec(chan.shape, lambda b: (0, 0)),
            pl.BlockSpec(pos.shape, lambda b: (0, 0)),
        ],
        out_specs=[tok(w) for w in widths],
        out_shape=[jax.ShapeDtypeStruct((batch * n_ctx, w), BF16) for w in widths],
        compiler_params=_params(16 * MIB, 1),
        name="ctx_mixer",
    )(sink, qa, ka, va, qb, kvb, fu, chan, pos)


def _outproj_kernel(x_ref, mod_ref, gpost_ref, oa_ref, ob_ref, oc_ref, w_ref, o_ref):
    gate = mod_ref[0, 5:6, :]
    y = _dot(oa_ref[...], w_ref[0:NA_WIDTH, :])
    y = y + _dot(ob_ref[...], w_ref[NA_WIDTH:NA_WIDTH + SWA_WIDTH, :])
    y = y + _dot(oc_ref[...], w_ref[NA_WIDTH + SWA_WIDTH:, :])
    o_ref[...] = x_ref[...] + gate * _rms(y, gpost_ref[1:2, :])


def _outproj(x, mod, g_post, oa, ob, oc, w_out, tokens_per_mod_row):
    n_tok = x.shape[0]
    tiles_per_row = tokens_per_mod_row // TOKEN_TILE
    tok = lambda w: pl.BlockSpec((TOKEN_TILE, w), lambda i: (i, 0))
    vmem = 2 * D_MODEL * D_MODEL * 2 + 8 * TOKEN_TILE * D_MODEL * 4 + 8 * MIB
    return pl.pallas_call(
        _outproj_kernel,
        grid=(n_tok // TOKEN_TILE,),
        in_specs=[
            tok(D_MODEL),
            pl.BlockSpec((1, N_MOD, D_MODEL), lambda i: (i // tiles_per_row, 0, 0)),
            pl.BlockSpec((N_SUB, D_MODEL), lambda i: (0, 0)),
            tok(NA_WIDTH), tok(SWA_WIDTH), tok(FNET_WIDTH),
            pl.BlockSpec((D_MODEL, D_MODEL), lambda i: (0, 0)),
        ],
        out_specs=tok(D_MODEL),
        out_shape=jax.ShapeDtypeStruct((n_tok, D_MODEL), F32),
        compiler_params=_params(vmem, 1),
        name="outproj",
    )(x, mod, g_post, oa, ob, oc, w_out)


def kernel(x, c, ctx, c_ctx, w_mod, b_mod, g_pre, g_post, w_ffn_in, w_ffn_out, w_in, w_out, na_rpb, swa_sink):
    batch, seq, _ = x.shape
    n_ctx = ctx.shape[1]
    depth = w_mod.shape[0]
    assert seq % (NA_ROWS_PER_STEP * GRID_W) == 0 and seq % TOKEN_TILE == 0
    assert (batch * n_ctx) % TOKEN_TILE == 0

    mod_rows = -(-(batch + 1) // 8) * 8
    c_rows = jnp.concatenate(
        [c, c_ctx[None, :], jnp.zeros((mod_rows - batch - 1, D_MODEL), F32)], axis=0)
    mod_all = _modulation(c_rows, w_mod, b_mod).reshape(depth, mod_rows, N_MOD, D_MODEL)

    rope_tables = _rope_tables(seq)
    xl = x.reshape(batch * seq, D_MODEL)
    xc = ctx.reshape(batch * n_ctx, D_MODEL)
    n_ctx_tok = batch * n_ctx

    for layer in range(depth):
        mod_l = mod_all[layer, :batch]
        mod_c = mod_all[layer, batch:batch + 1]
        gp, gq = g_pre[layer], g_post[layer]
        w1a, w1b = w_ffn_in[layer, 0].astype(BF16), w_ffn_in[layer, 1].astype(BF16)
        w2a, w2b = w_ffn_out[layer, 0].astype(BF16), w_ffn_out[layer, 1].astype(BF16)
        wi, wo = w_in[layer].astype(BF16), w_out[layer].astype(BF16)
        sink = swa_sink[layer].astype(F32)
        last = layer == depth - 1

        xl = _ffn(xl, mod_l, gp, gq, w1a, w2a, 0, seq)
        xc = _ffn(xc, mod_c, gp, gq, w1a, w2a, 0, n_ctx_tok)

        qa, ka, va, qb, kvb, fu = _inproj(xl, mod_l, gp, wi, seq, rope_tables)
        qac, kac, vac, qbc, kvbc, fuc = _inproj(xc, mod_c, gp, wi, n_ctx_tok, None)

        bias_tab = _na_bias_tables(na_rpb[layer], seq // GRID_W)
        oa = _na_attention(qa, ka, va, kac, vac, bias_tab, batch, seq, n_ctx)
        ob = _swa_attention(qb, kvb, kvbc, sink, batch, seq, n_ctx)
        oc = _fnet(fu, batch, seq)
        xl = _outproj(xl, mod_l, gq, oa, ob, oc, wo, seq)
        xl = _ffn(xl, mod_l, gp, gq, w1b, w2b, 2, seq)

        if not last:
            oac, obc, occ = _ctx_mixer(qac, kac, vac, qbc, kvbc, fuc, sink, batch, n_ctx)
            xc = _outproj(xc, mod_c, gq, oac, obc, occ, wo, n_ctx_tok)
            xc = _ffn(xc, mod_c, gp, gq, w1b, w2b, 2, n_ctx_tok)

    return xl.reshape(batch, seq, D_MODEL)
```

```python
import functools

import numpy as np
import jax
import jax.numpy as jnp
from jax import lax
from jax.experimental import pallas as pl
from jax.experimental.pallas import tpu as pltpu

F32 = jnp.float32
BF16 = jnp.bfloat16

D_MODEL = 1024
GRID_W = 64
HEAD_DIM = 64
NA_HEADS = 6
NA_KR = 8
NA_KW = 16
SWA_HEADS = 6
SWA_KV_HEADS = 2
SWA_WINDOW = 128
SWA_BLOCK = 128
FNET_GROUPS = 4
FNET_GROUP_DIM = 64
NA_WIDTH = NA_HEADS * HEAD_DIM
SWA_WIDTH = SWA_HEADS * HEAD_DIM
SWA_KV_WIDTH = SWA_KV_HEADS * HEAD_DIM
FNET_WIDTH = FNET_GROUPS * FNET_GROUP_DIM
D_IN = 3 * NA_WIDTH + SWA_WIDTH + 2 * SWA_KV_WIDTH + FNET_WIDTH
D_FF = 256 * ((8 * D_MODEL // 3 + 255) // 256)
N_SUB = 3
N_MOD = 3 * N_SUB
MACARON_WEIGHT = 0.5
ROPE_BASE = 10000.0
RMS_EPS = 1e-6
NEG_INF = -1e30
QK_SCALE = HEAD_DIM ** -0.5

V7X_LANES = 128
V7X_VMEM_LIMIT_BYTES = 60000 * 1024
MIB = 1024 * 1024

TOKEN_TILE = 512
FFN_CHUNK = 256
MOD_TILE = 1152
NA_ROWS_PER_STEP = 8
NA_PAIR_ROWS = 2
NA_PAIR_KEY_ROWS = 10
FFT_N1 = 16
FFT_N2 = 256


def _params(vmem_bytes, n_grid):
    limit = int(min(max(vmem_bytes, 16 * MIB), V7X_VMEM_LIMIT_BYTES))
    return pltpu.CompilerParams(dimension_semantics=("arbitrary",) * n_grid, vmem_limit_bytes=limit)


def _rms(x, g):
    return x * lax.rsqrt(jnp.mean(x * x, axis=-1, keepdims=True) + RMS_EPS) * g


def _nt_dot(a, b):
    return lax.dot_general(a, b, (((1,), (1,)), ((), ())), preferred_element_type=F32)


def _dot(a, b):
    return jnp.dot(a, b, preferred_element_type=F32)


def _mod_kernel(c_ref, w_ref, b_ref, o_ref):
    a = c_ref[...]
    act = a / (1.0 + jnp.exp(-a))
    o_ref[0] = _dot(act.astype(BF16), w_ref[0].astype(BF16)) + b_ref[0]


def _modulation(c_rows, w_mod, b_mod):
    depth, _, n_out = w_mod.shape
    rows = c_rows.shape[0]
    return pl.pallas_call(
        _mod_kernel,
        grid=(depth, n_out // MOD_TILE),
        in_specs=[
            pl.BlockSpec((rows, D_MODEL), lambda l, j: (0, 0)),
            pl.BlockSpec((1, D_MODEL, MOD_TILE), lambda l, j: (l, 0, j)),
            pl.BlockSpec((1, 1, MOD_TILE), lambda l, j: (l, 0, j)),
        ],
        out_specs=pl.BlockSpec((1, rows, MOD_TILE), lambda l, j: (l, 0, j)),
        out_shape=jax.ShapeDtypeStruct((depth, rows, n_out), F32),
        compiler_params=_params(4 * D_MODEL * MOD_TILE * 4, 2),
        name="modulation",
    )(c_rows, w_mod, b_mod.reshape(depth, 1, n_out))


def _ffn_kernel(x_ref, mod_ref, gpre_ref, gpost_ref, w1_ref, w2_ref, o_ref, act_ref, *, sub):
    x = x_ref[...]
    shift = mod_ref[0, 3 * sub:3 * sub + 1, :]
    scale = mod_ref[0, 3 * sub + 1:3 * sub + 2, :]
    gate = mod_ref[0, 3 * sub + 2:3 * sub + 3, :]
    h = _rms(x, gpre_ref[sub:sub + 1, :]) * (1.0 + scale) + shift
    hb = h.astype(BF16)
    for c in range(D_FF // FFN_CHUNK):
        lo = c * FFN_CHUNK
        g = _dot(hb, w1_ref[:, lo:lo + FFN_CHUNK])
        u = _dot(hb, w1_ref[:, D_FF + lo:D_FF + lo + FFN_CHUNK])
        act_ref[:, lo:lo + FFN_CHUNK] = (g / (1.0 + jnp.exp(-g)) * u).astype(BF16)
    y = _dot(act_ref[...], w2_ref[...])
    o_ref[...] = x + MACARON_WEIGHT * gate * _rms(y, gpost_ref[sub:sub + 1, :])


def _ffn(x, mod, g_pre, g_post, w1, w2, sub, tokens_per_mod_row):
    n_tok = x.shape[0]
    tiles_per_row = tokens_per_mod_row // TOKEN_TILE
    vmem = (2 * (D_MODEL * 2 * D_FF + D_FF * D_MODEL) * 2 + 4 * TOKEN_TILE * D_MODEL * 4
            + TOKEN_TILE * D_FF * 2 + 8 * MIB)
    return pl.pallas_call(
        functools.partial(_ffn_kernel, sub=sub),
        grid=(n_tok // TOKEN_TILE,),
        in_specs=[
            pl.BlockSpec((TOKEN_TILE, D_MODEL), lambda i: (i, 0)),
            pl.BlockSpec((1, N_MOD, D_MODEL), lambda i: (i // tiles_per_row, 0, 0)),
            pl.BlockSpec((N_SUB, D_MODEL), lambda i: (0, 0)),
            pl.BlockSpec((N_SUB, D_MODEL), lambda i: (0, 0)),
            pl.BlockSpec((D_MODEL, 2 * D_FF), lambda i: (0, 0)),
            pl.BlockSpec((D_FF, D_MODEL), lambda i: (0, 0)),
        ],
        out_specs=pl.BlockSpec((TOKEN_TILE, D_MODEL), lambda i: (i, 0)),
        out_shape=jax.ShapeDtypeStruct((n_tok, D_MODEL), F32),
        scratch_shapes=[pltpu.VMEM((TOKEN_TILE, D_FF), BF16)],
        compiler_params=_params(vmem, 1),
        name="ffn",
    )(x, mod, g_pre, g_post, w1, w2)


_KVB_WIDTH = 4 * SWA_KV_WIDTH


def _inproj_kernel(*refs, latent):
    if latent:
        (x_ref, mod_ref, gpre_ref, w_ref, cos_ref, sina_ref, sinb_ref,
         qa_ref, ka_ref, va_ref, qb_ref, kvb_ref, fu_ref, fu_scr) = refs
    else:
        (x_ref, mod_ref, gpre_ref, w_ref,
         qa_ref, ka_ref, va_ref, qb_ref, kvb_ref, fu_ref) = refs
    x = x_ref[...]
    shift = mod_ref[0, 3:4, :]
    scale = mod_ref[0, 4:5, :]
    h = _rms(x, gpre_ref[1:2, :]) * (1.0 + scale) + shift
    r = _dot(h.astype(BF16), w_ref[...])
    o = 0
    qa_ref[...] = (r[:, o:o + NA_WIDTH] * QK_SCALE).astype(BF16)
    o += NA_WIDTH
    ka_ref[...] = r[:, o:o + NA_WIDTH].astype(BF16)
    o += NA_WIDTH
    va_ref[...] = r[:, o:o + NA_WIDTH].astype(BF16)
    o += NA_WIDTH
    qk = r[:, o:o + SWA_WIDTH + SWA_KV_WIDTH]
    o += SWA_WIDTH + SWA_KV_WIDTH
    if latent:
        reps =(SWA_WIDTH + SWA_KV_WIDTH) // V7X_LANES
        width = qk.shape[1]
        cos = jnp.concatenate([cos_ref[...]] * reps, axis=1)
        sina = jnp.concatenate([sina_ref[...]] * reps, axis=1)
        sinb = jnp.concatenate([sinb_ref[...]] * reps, axis=1)
        half = HEAD_DIM // 4
        qk = (qk * cos + pltpu.roll(qk, width - half, axis=1) * sina
              + pltpu.roll(qk, half, axis=1) * sinb)
    qb_ref[...] = (qk[:, :SWA_WIDTH] * QK_SCALE).astype(BF16)
    kb = qk[:, SWA_WIDTH:]
    vb = r[:, o:o + SWA_KV_WIDTH]
    o += SWA_KV_WIDTH
    kvb_ref[:, 0:SWA_KV_WIDTH] = kb.astype(BF16)
    kvb_ref[:, SWA_KV_WIDTH:2 * SWA_KV_WIDTH] = pltpu.roll(kb, HEAD_DIM, axis=1).astype(BF16)
    kvb_ref[:, 2 * SWA_KV_WIDTH:3 * SWA_KV_WIDTH] = vb.astype(BF16)
    kvb_ref[:, 3 * SWA_KV_WIDTH:4 * SWA_KV_WIDTH] = pltpu.roll(vb, HEAD_DIM, axis=1).astype(BF16)
    fu = r[:, o:o + FNET_WIDTH]
    if latent:
        rows = TOKEN_TILE // FFT_N1
        for blk in range(FNET_WIDTH // V7X_LANES):
            fu_scr[blk] = fu[:, blk * V7X_LANES:(blk + 1) * V7X_LANES]
        for n1 in range(FFT_N1):
            for blk in range(FNET_WIDTH // V7X_LANES):
                lo = n1 * FNET_WIDTH + blk * V7X_LANES
                fu_ref[0, :, lo:lo + V7X_LANES] = (
                    fu_scr[blk, pl.ds(n1, rows, stride=FFT_N1), :].astype(BF16))
    else:
        fu_ref[...] = fu.astype(BF16)


def _rope_tables(seq):
    t = jnp.arange(seq)
    rows = (t // GRID_W).astype(F32)
    cols = (t % GRID_W).astype(F32)
    half = HEAD_DIM // 4
    inv = ROPE_BASE ** (-jnp.arange(half, dtype=F32) / half)
    ang_r = rows[:, None] * inv[None, :]
    ang_c = cols[:, None] * inv[None, :]
    cos_h = jnp.concatenate([jnp.cos(ang_r)] * 2 + [jnp.cos(ang_c)] * 2, axis=1)
    sin_h = jnp.concatenate([jnp.sin(ang_r)] * 2 + [jnp.sin(ang_c)] * 2, axis=1)
    first = (np.arange(HEAD_DIM) % (2 * half)) < half
    sina_h = jnp.where(first[None, :], -sin_h, 0.0)
    sinb_h = jnp.where(first[None, :], 0.0, sin_h)
    two = lambda a: jnp.concatenate([a, a], axis=1)
    return two(cos_h), two(sina_h), two(sinb_h)


def _inproj(x, mod, g_pre, w_in, tokens_per_mod_row, rope_tables):
    n_tok = x.shape[0]
    tiles_per_row = tokens_per_mod_row // TOKEN_TILE
    latent = rope_tables is not None
    in_specs = [
        pl.BlockSpec((TOKEN_TILE, D_MODEL), lambda i: (i, 0)),
        pl.BlockSpec((1, N_MOD, D_MODEL), lambda i: (i // tiles_per_row, 0, 0)),
        pl.BlockSpec((N_SUB, D_MODEL), lambda i: (0, 0)),
        pl.BlockSpec((D_MODEL, D_IN), lambda i: (0, 0)),
    ]
    args = [x, mod, g_pre, w_in]
    widths = (NA_WIDTH, NA_WIDTH, NA_WIDTH, SWA_WIDTH, _KVB_WIDTH)
    out_specs = [pl.BlockSpec((TOKEN_TILE, w), lambda i: (i, 0)) for w in widths]
    out_shape = [jax.ShapeDtypeStruct((n_tok, w), BF16) for w in widths]
    scratch = []
    if latent:
        for tab in rope_tables:
            in_specs.append(pl.BlockSpec((TOKEN_TILE, V7X_LANES), lambda i: (i % tiles_per_row, 0)))
            args.append(tab)
        rows = TOKEN_TILE // FFT_N1
        out_specs.append(pl.BlockSpec((1, rows, FFT_N1 * FNET_WIDTH),
                                      lambda i: (i // tiles_per_row, i % tiles_per_row, 0)))
        out_shape.append(jax.ShapeDtypeStruct(
            (n_tok // tokens_per_mod_row, tokens_per_mod_row // FFT_N1, FFT_N1 * FNET_WIDTH), BF16))
        scratch.append(pltpu.VMEM((FNET_WIDTH // V7X_LANES, TOKEN_TILE, V7X_LANES), F32))
    else:
        out_specs.append(pl.BlockSpec((TOKEN_TILE, FNET_WIDTH), lambda i: (i, 0)))
        out_shape.append(jax.ShapeDtypeStruct((n_tok, FNET_WIDTH), BF16))
    vmem = 2 * D_MODEL * D_IN * 2 + 4 * TOKEN_TILE * D_MODEL * 4 + 6 * TOKEN_TILE * D_IN * 4 + 8 * MIB
    return pl.pallas_call(
        functools.partial(_inproj_kernel, latent=latent),
        grid=(n_tok // TOKEN_TILE,),
        in_specs=in_specs,
        out_specs=out_specs,
        out_shape=out_shape,
        scratch_shapes=scratch,
        compiler_params=_params(vmem, 1),
        name="inproj_latent" if latent else "inproj_ctx",
    )(*args)


def _half_mask(rows, half):
    lane = lax.broadcasted_iota(jnp.int32, (rows, V7X_LANES), 1)
    return (lane >= HEAD_DIM) if half else (lane < HEAD_DIM)


def _attend(q, keys, values, biases, extra_logit=None):
    scores = []
    for k, b in zip(keys, biases):
        s = _nt_dot(q, k)
        scores.append(s if b is None else s + b)
    m = scores[0].max(axis=-1, keepdims=True)
    for s in scores[1:]:
        m = jnp.maximum(m, s.max(axis=-1, keepdims=True))
    if extra_logit is not None:
        m = jnp.maximum(m, extra_logit)
    denom = None
    out = None
    for s, v in zip(scores, values):
        p = jnp.exp(s - m)
        d = p.sum(axis=-1, keepdims=True)
        denom = d if denom is None else denom + d
        pv = _dot(p.astype(BF16), v)
        out = pv if out is None else out + pv
    if extra_logit is not None:
        denom = denom + jnp.exp(extra_logit - m)
    return out / denom


_NA_PAIRS_PER_STEP = NA_ROWS_PER_STEP // NA_PAIR_ROWS
_NA_Q = NA_PAIR_ROWS * GRID_W
_NA_K = NA_PAIR_KEY_ROWS * GRID_W


def _na_kernel(q_ref, k_ref, v_ref, kc_ref, vc_ref, bias_ref, o_ref, *, grid_rows):
    j = pl.program_id(2)
    kc = kc_ref[...]
    vc = vc_ref[...]
    for t in range(_NA_PAIRS_PER_STEP):
        pair = j * _NA_PAIRS_PER_STEP + t
        start_row = jnp.clip(NA_PAIR_ROWS * pair - NA_KR // 2, 0, grid_rows - NA_PAIR_KEY_ROWS)
        start = pl.multiple_of(start_row * GRID_W, 2 * GRID_W)
        q = q_ref[t * _NA_Q:(t + 1) * _NA_Q, :]
        k = k_ref[pl.ds(start, _NA_K), :]
        v = v_ref[pl.ds(start, _NA_K), :]
        res = []
        for half in range(2):
            qm = jnp.where(_half_mask(_NA_Q, half), q, jnp.zeros_like(q))
            res.append(_attend(qm, (k, kc), (v, vc), (bias_ref[0, half, t], None)))
        o_ref[t * _NA_Q:(t + 1) * _NA_Q, :] = jnp.where(
            _half_mask(_NA_Q, 0), res[0], res[1]).astype(BF16)


def _na_bias_tables(rpb, grid_rows):
    n_blocks = grid_rows // NA_ROWS_PER_STEP
    cases = (0, min(1, n_blocks - 1), n_blocks - 1)
    kr = min(NA_KR, grid_rows)
    ro = np.zeros((3, _NA_PAIRS_PER_STEP, NA_PAIR_ROWS, NA_PAIR_KEY_ROWS), np.int32)
    rvalid = np.zeros(ro.shape, bool)
    for ci, jb in enumerate(cases):
        for t in range(_NA_PAIRS_PER_STEP):
            pair = jb * _NA_PAIRS_PER_STEP + t
            start_row = int(np.clip(NA_PAIR_ROWS * pair - NA_KR // 2, 0, grid_rows - NA_PAIR_KEY_ROWS))
            for l in range(NA_PAIR_ROWS):
                rq = NA_PAIR_ROWS * pair + l
                rs = int(np.clip(rq - kr // 2, 0, grid_rows - kr))
                for i in range(NA_PAIR_KEY_ROWS):
                    rk = start_row + i
                    rvalid[ci, t, l, i] = rs <= rk < rs + kr
                    ro[ci, t, l, i] = int(np.clip(rk - rq + NA_KR - 1, 0, 2 * NA_KR - 2))
    n_dr, n_dc = rpb.shape[1], rpb.shape[2]
    return pl.pallas_call(
        functools.partial(_na_bias_kernel, ro=ro, rvalid=rvalid, n_dr=n_dr, n_dc=n_dc),
        grid=(NA_HEADS,),
        in_specs=[pl.BlockSpec(memory_space=pltpu.SMEM)],
        out_specs=pl.BlockSpec((3, 1, _NA_PAIRS_PER_STEP, _NA_Q, _NA_K), lambda h: (0, h, 0, 0, 0)),
        out_shape=jax.ShapeDtypeStruct((3, NA_HEADS, _NA_PAIRS_PER_STEP, _NA_Q, _NA_K), F32),
        scratch_shapes=[pltpu.VMEM((n_dr, GRID_W, GRID_W), F32)],
        compiler_params=_params(4 * 3 * _NA_PAIRS_PER_STEP * _NA_Q * _NA_K * 4, 1),
        name="na_bias",
    )(rpb.astype(F32).reshape(-1))


def _na_bias_kernel(rpb_ref, o_ref, toep_ref, *, ro, rvalid, n_dr, n_dc):
    h = pl.program_id(0)
    cq = lax.broadcasted_iota(jnp.int32, (GRID_W, GRID_W), 0)
    ck = lax.broadcasted_iota(jnp.int32, (GRID_W, GRID_W), 1)
    ws = jnp.clip(cq - NA_KW // 2, 0, GRID_W - NA_KW)
    cvalid = (ck >= ws) & (ck < ws + NA_KW)
    co = ck - cq + NA_KW - 1
    neg = jnp.full((GRID_W, GRID_W), NEG_INF, F32)
    for dr in range(n_dr):
        tile = neg
        for dc in range(n_dc):
            tile = jnp.where(co == dc, rpb_ref[(h * n_dr + dr) * n_dc + dc], tile)
        toep_ref[dr] = jnp.where(cvalid, tile, neg)
    n_cases, n_pairs, n_l, n_i = ro.shape
    for c in range(n_cases):
        for t in range(n_pairs):
            for l in range(n_l):
                for i in range(n_i):
                    tile = toep_ref[int(ro[c, t, l, i])] if rvalid[c, t, l, i] else neg
                    o_ref[c, 0, t, l * GRID_W:(l + 1) * GRID_W, i * GRID_W:(i + 1) * GRID_W] = tile


def _na_attention(qa, ka, va, kac, vac, bias_tab, batch, seq, n_ctx):
    grid_rows = seq // GRID_W
    n_blocks = grid_rows // NA_ROWS_PER_STEP
    rows_q = NA_ROWS_PER_STEP * GRID_W
    n_pairs = NA_HEADS // 2

    def case_of(j):
        return jnp.where(j == 0, 0, jnp.where(j == n_blocks - 1, 2, 1))

    bias_block = (1, 2, _NA_PAIRS_PER_STEP, _NA_Q, _NA_K)
    vmem = (2 * 2 * _NA_PAIRS_PER_STEP * _NA_Q * _NA_K * 4 + 4 * seq * V7X_LANES * 2 + 16 * MIB)
    return pl.pallas_call(
        functools.partial(_na_kernel, grid_rows=grid_rows),
        grid=(batch, n_pairs, n_blocks),
        in_specs=[
            pl.BlockSpec((rows_q, V7X_LANES), lambda b, p, j: (b * n_blocks + j, p)),
            pl.BlockSpec((seq, V7X_LANES), lambda b, p, j: (b, p)),
            pl.BlockSpec((seq, V7X_LANES), lambda b, p, j: (b, p)),
            pl.BlockSpec((n_ctx, V7X_LANES), lambda b, p, j: (b, p)),
            pl.BlockSpec((n_ctx, V7X_LANES), lambda b, p, j: (b, p)),
            pl.BlockSpec(bias_block, lambda b, p, j: (case_of(j), p, 0, 0, 0)),
        ],
        out_specs=pl.BlockSpec((rows_q, V7X_LANES), lambda b, p, j: (b * n_blocks + j, p)),
        out_shape=jax.ShapeDtypeStruct((batch * seq, NA_WIDTH), BF16),
        compiler_params=_params(vmem, 3),
        name="na_attention",
    )(qa, ka, va, kac, vac, bias_tab)


_SWA_SPAN = 3 * SWA_BLOCK
_SWA_GROUP = SWA_HEADS // SWA_KV_HEADS


def _swa_kv_slot(head):
    return (head // _SWA_GROUP) ^ (head % 2)


def _swa_kernel(sink_ref, q_ref, kv_ref, kvc_ref, mask_ref, o_ref, *, seq):
    n = pl.program_id(1)
    start = pl.multiple_of(jnp.clip((n - 1) * SWA_BLOCK, 0, seq - _SWA_SPAN), SWA_BLOCK)
    mask = mask_ref[0]
    for blk in range(SWA_HEADS // 2):
        q = q_ref[:, blk * V7X_LANES:(blk + 1) * V7X_LANES]
        res = []
        for half in range(2):
            head = 2 * blk + half
            slot = _swa_kv_slot(head)
            k = kv_ref[pl.ds(start, _SWA_SPAN), slot * V7X_LANES:(slot + 1) * V7X_LANES]
            v = kv_ref[pl.ds(start, _SWA_SPAN), (2 + slot) * V7X_LANES:(3 + slot) * V7X_LANES]
            kc = kvc_ref[:, slot * V7X_LANES:(slot + 1) * V7X_LANES]
            vc = kvc_ref[:, (2 + slot) * V7X_LANES:(3 + slot) * V7X_LANES]
            qm = jnp.where(_half_mask(SWA_BLOCK, half), q, jnp.zeros_like(q))
            sink = jnp.full((SWA_BLOCK, 1), sink_ref[head], F32)
            res.append(_attend(qm, (k, kc), (v, vc), (mask, None), extra_logit=sink))
        o_ref[:, blk * V7X_LANES:(blk + 1) * V7X_LANES] = jnp.where(
            _half_mask(SWA_BLOCK, 0), res[0], res[1]).astype(BF16)


def _swa_masks(seq):
    nb = seq // SWA_BLOCK
    out = np.zeros((3, SWA_BLOCK, _SWA_SPAN), np.float32)
    for ci, n in enumerate((0, min(1, nb - 1), nb - 1)):
        start = int(np.clip((n - 1) * SWA_BLOCK, 0, seq - _SWA_SPAN))
        kpos = start + np.arange(_SWA_SPAN)[None, :]
        qpos = n * SWA_BLOCK + np.arange(SWA_BLOCK)[:, None]
        out[ci] = np.where(np.abs(kpos - qpos) <= SWA_WINDOW, 0.0, NEG_INF)
    return jnp.asarray(out)


def _swa_attention(qb, kvb, kvbc, sink, batch, seq, n_ctx):
    nb = seq // SWA_BLOCK

    def case_of(n):
        return jnp.where(n == 0, 0, jnp.where(n == nb - 1, 2, 1))

    vmem = 4 * seq * _KVB_WIDTH * 2 + 16 * MIB
    return pl.pallas_call(
        functools.partial(_swa_kernel, seq=seq),
        grid=(batch, nb),
        in_specs=[
            pl.BlockSpec(memory_space=pltpu.SMEM),
            pl.BlockSpec((SWA_BLOCK, SWA_WIDTH), lambda b, n: (b * nb + n, 0)),
            pl.BlockSpec((seq, _KVB_WIDTH), lambda b, n: (b, 0)),
            pl.BlockSpec((n_ctx, _KVB_WIDTH), lambda b, n: (b, 0)),
            pl.BlockSpec((1, SWA_BLOCK, _SWA_SPAN), lambda b, n: (case_of(n), 0, 0)),
        ],
        out_specs=pl.BlockSpec((SWA_BLOCK, SWA_WIDTH), lambda b, n: (b * nb + n, 0)),
        out_shape=jax.ShapeDtypeStruct((batch * seq, SWA_WIDTH), BF16),
        compiler_params=_params(vmem, 2),
        name="swa_attention",
    )(sink, qb, kvb, kvbc, _swa_masks(seq))


def _dft_constants():
    jc = np.arange(FNET_GROUP_DIM)
    ang_c = 2.0 * np.pi * ((jc[:, None] * jc[None, :]) % FNET_GROUP_DIM) / FNET_GROUP_DIM
    eye = np.eye(FNET_GROUPS)
    bd_cos = np.kron(eye, np.cos(ang_c))
    bd_sin = np.kron(eye, np.sin(ang_c))
    chan = np.concatenate([bd_cos, -bd_sin], axis=1)
    jp = np.arange(FFT_N2)
    ang_p = 2.0 * np.pi * ((jp[:, None] * jp[None, :]) % FFT_N2) / FFT_N2
    cp, sp = np.cos(ang_p), np.sin(ang_p)
    pos = np.block([[cp, sp], [-sp, cp]])
    return jnp.asarray(chan, F32), jnp.asarray(pos, F32)


def _twiddle_tables(n_total):
    n1 = np.arange(FFT_N1)[:, None]
    k2 = np.arange(FFT_N2)[None, :]
    ang = 2.0 * np.pi * ((n1 * k2) % n_total) / n_total
    bc = lambda a: jnp.asarray(np.repeat(a[:, :, None], V7X_LANES, axis=2), F32)
    return bc(np.cos(ang)), bc(np.sin(ang))


def _dft256_of_channels(u, chan, pos):
    z = _dot(u, chan)
    zs = jnp.concatenate([z[:, :FNET_WIDTH], z[:, FNET_WIDTH:]], axis=0).astype(BF16)
    a = _dot(pos, zs)
    return a[:FFT_N2], a[FFT_N2:]


def _fnet_kernel(u_ref, chan_ref, pos_ref, twc_ref, tws_ref, o_ref, bre_ref, bim_ref, *, scale):
    chan = chan_ref[...].astype(BF16)
    pos = pos_ref[...].astype(BF16)
    for n1 in range(FFT_N1):
        u = u_ref[0, :, n1 * FNET_WIDTH:(n1 + 1) * FNET_WIDTH]
        a_re, a_im = _dft256_of_channels(u, chan, pos)
        if n1 == 0:
            bre_ref[0] = a_re
            bim_ref[0] = a_im
        else:
            c = jnp.concatenate([twc_ref[n1]] * (FNET_WIDTH // V7X_LANES), axis=1)
            s = jnp.concatenate([tws_ref[n1]] * (FNET_WIDTH // V7X_LANES), axis=1)
            bre_ref[n1] = a_re * c + a_im * s
            bim_ref[n1] = a_im * c - a_re * s
    for k1 in range(FFT_N1):
        acc = bre_ref[0]
        for n1 in range(1, FFT_N1):
            m = (n1 * k1) % FFT_N1
            c = float(np.cos(2.0 * np.pi * m / FFT_N1))
            s = float(np.sin(2.0 * np.pi * m / FFT_N1))
            if abs(c) > 1e-9:
                acc = acc + c * bre_ref[n1]
            if abs(s) > 1e-9:
                acc = acc + s * bim_ref[n1]
        o_ref[k1 * FFT_N2:(k1 + 1) * FFT_N2, :] = (acc * scale).astype(BF16)


def _fnet(fu, batch, seq):
    assert seq == FFT_N1 * FFT_N2
    chan, pos = _dft_constants()
    twc, tws = _twiddle_tables(seq)
    scale = float((seq * FNET_GROUP_DIM) ** -0.5)
    assert fu.shape == (batch, FFT_N2, FFT_N1 * FNET_WIDTH)
    u = fu
    vmem = 2 * FFT_N1 * FFT_N2 * FNET_WIDTH * 4 + 8 * seq * FNET_WIDTH * 2 + 16 * MIB
    return pl.pallas_call(
        functools.partial(_fnet_kernel, scale=scale),
        grid=(batch,),
        in_specs=[
            pl.BlockSpec((1, FFT_N2, FFT_N1 * FNET_WIDTH), lambda b: (b, 0, 0)),
            pl.BlockSpec(chan.shape, lambda b: (0, 0)),
            pl.BlockSpec(pos.shape, lambda b: (0, 0)),
            pl.BlockSpec(twc.shape, lambda b: (0, 0, 0)),
            pl.BlockSpec(tws.shape, lambda b: (0, 0, 0)),
        ],
        out_specs=pl.BlockSpec((seq, FNET_WIDTH), lambda b: (b, 0)),
        out_shape=jax.ShapeDtypeStruct((batch * seq, FNET_WIDTH), BF16),
        scratch_shapes=[pltpu.VMEM((FFT_N1, FFT_N2, FNET_WIDTH), F32),
                        pltpu.VMEM((FFT_N1, FFT_N2, FNET_WIDTH), F32)],
        compiler_params=_params(vmem, 1),
        name="fnet",
    )(u, chan, pos, twc, tws)


def _ctx_mixer_kernel(sink_ref, qa_ref, ka_ref, va_ref, qb_ref, kvb_ref, fu_ref, chan_ref, pos_ref,
                      oa_ref, ob_ref, oc_ref, *, n_ctx, scale):
    for blk in range(NA_HEADS // 2):
        sl = slice(blk * V7X_LANES, (blk + 1) * V7X_LANES)
        q, k, v = qa_ref[:, sl], ka_ref[:, sl], va_ref[:, sl]
        res = []
        for half in range(2):
            qm = jnp.where(_half_mask(n_ctx, half), q, jnp.zeros_like(q))
            res.append(_attend(qm, (k,), (v,), (None,)))
        oa_ref[:, sl] = jnp.where(_half_mask(n_ctx, 0), res[0], res[1]).astype(BF16)
    for blk in range(SWA_HEADS // 2):
        sl = slice(blk * V7X_LANES, (blk + 1) * V7X_LANES)
        q = qb_ref[:, sl]
        res = []
        for half in range(2):
            head = 2 * blk + half
            slot = _swa_kv_slot(head)
            k = kvb_ref[:, slot * V7X_LANES:(slot + 1) * V7X_LANES]
            v = kvb_ref[:, (2 + slot) * V7X_LANES:(3 + slot) * V7X_LANES]
            qm = jnp.where(_half_mask(n_ctx, half), q, jnp.zeros_like(q))
            sink = jnp.full((n_ctx, 1), sink_ref[head], F32)
            res.append(_attend(qm, (k,), (v,), (None,), extra_logit=sink))
        ob_ref[:, sl] = jnp.where(_half_mask(n_ctx, 0), res[0], res[1]).astype(BF16)
    a_re, _ = _dft256_of_channels(fu_ref[...], chan_ref[...].astype(BF16), pos_ref[...].astype(BF16))
    oc_ref[...] = (a_re * scale).astype(BF16)


def _ctx_mixer(qa, ka, va, qb, kvb, fu, sink, batch, n_ctx):
    assert n_ctx == FFT_N2
    chan, pos = _dft_constants()
    scale = float((n_ctx * FNET_GROUP_DIM) ** -0.5)
    tok = lambda w: pl.BlockSpec((n_ctx, w), lambda b: (b, 0))
    widths = (NA_WIDTH, SWA_WIDTH, FNET_WIDTH)
    return pl.pallas_call(
        functools.partial(_ctx_mixer_kernel, n_ctx=n_ctx, scale=scale),
        grid=(batch,),
        in_specs=[
            pl.BlockSpec(memory_space=pltpu.SMEM),
            tok(NA_WIDTH), tok(NA_WIDTH), tok(NA_WIDTH), tok(SWA_WIDTH), tok(_KVB_WIDTH), tok(FNET_WIDTH),
            pl.BlockSpec(chan.shape, lambda b: (0, 0)),
            pl.BlockSpec(pos.shape, lambda b: (0, 0)),
        ],
        out_specs=[tok(w) for w in widths],
        out_shape=[jax.ShapeDtypeStruct((batch * n_ctx, w), BF16) for w in widths],
        compiler_params=_params(16 * MIB, 1),
        name="ctx_mixer",
    )(sink, qa, ka, va, qb, kvb, fu, chan, pos)


def _outproj_kernel(x_ref, mod_ref, gpost_ref, oa_ref, ob_ref, oc_ref, w_ref, o_ref):
    gate = mod_ref[0, 5:6, :]
    y = _dot(oa_ref[...], w_ref[0:NA_WIDTH, :])
    y = y + _dot(ob_ref[...], w_ref[NA_WIDTH:NA_WIDTH + SWA_WIDTH, :])
    y = y + _dot(oc_ref[...], w_ref[NA_WIDTH + SWA_WIDTH:, :])
    o_ref[...] = x_ref[...] + gate * _rms(y, gpost_ref[1:2, :])


def _outproj(x, mod, g_post, oa, ob, oc, w_out, tokens_per_mod_row):
    n_tok = x.shape[0]
    tiles_per_row = tokens_per_mod_row // TOKEN_TILE
    tok = lambda w: pl.BlockSpec((TOKEN_TILE, w), lambda i: (i, 0))
    vmem = 2 * D_MODEL * D_MODEL * 2 + 8 * TOKEN_TILE * D_MODEL * 4 + 8 * MIB
    return pl.pallas_call(
        _outproj_kernel,
        grid=(n_tok // TOKEN_TILE,),
        in_specs=[
            tok(D_MODEL),
            pl.BlockSpec((1, N_MOD, D_MODEL), lambda i: (i // tiles_per_row, 0, 0)),
            pl.BlockSpec((N_SUB, D_MODEL), lambda i: (0, 0)),
            tok(NA_WIDTH), tok(SWA_WIDTH), tok(FNET_WIDTH),
            pl.BlockSpec((D_MODEL, D_MODEL), lambda i: (0, 0)),
        ],
        out_specs=tok(D_MODEL),
        out_shape=jax.ShapeDtypeStruct((n_tok, D_MODEL), F32),
        compiler_params=_params(vmem, 1),
        name="outproj",
    )(x, mod, g_post, oa, ob, oc, w_out)


def kernel(x, c, ctx, c_ctx, w_mod, b_mod, g_pre, g_post, w_ffn_in, w_ffn_out, w_in, w_out, na_rpb, swa_sink):
    batch, seq, _ = x.shape
    n_ctx = ctx.shape[1]
    depth = w_mod.shape[0]
    assert seq % (NA_ROWS_PER_STEP * GRID_W) == 0 and seq % TOKEN_TILE == 0
    assert (batch * n_ctx) % TOKEN_TILE == 0

    mod_rows = -(-(batch + 1) // 8) * 8
    c_rows = jnp.concatenate(
        [c, c_ctx[None, :], jnp.zeros((mod_rows - batch - 1, D_MODEL), F32)], axis=0)
    mod_all = _modulation(c_rows, w_mod, b_mod).reshape(depth, mod_rows, N_MOD, D_MODEL)

    rope_tables = _rope_tables(seq)
    xl = x.reshape(batch * seq, D_MODEL)
    xc = ctx.reshape(batch * n_ctx, D_MODEL)
    n_ctx_tok = batch * n_ctx

    for layer in range(depth):
        mod_l = mod_all[layer, :batch]
        mod_c = mod_all[layer, batch:batch + 1]
        gp, gq = g_pre[layer], g_post[layer]
        w1a, w1b = w_ffn_in[layer, 0].astype(BF16), w_ffn_in[layer, 1].astype(BF16)
        w2a, w2b = w_ffn_out[layer, 0].astype(BF16), w_ffn_out[layer, 1].astype(BF16)
        wi, wo = w_in[layer].astype(BF16), w_out[layer].astype(BF16)
        sink = swa_sink[layer].astype(F32)
        last = layer == depth - 1

        xl = _ffn(xl, mod_l, gp, gq, w1a, w2a, 0, seq)
        xc = _ffn(xc, mod_c, gp, gq, w1a, w2a, 0, n_ctx_tok)

        qa, ka, va, qb, kvb, fu = _inproj(xl, mod_l, gp, wi, seq, rope_tables)
        qac, kac, vac, qbc, kvbc, fuc = _inproj(xc, mod_c, gp, wi, n_ctx_tok, None)

        bias_tab = _na_bias_tables(na_rpb[layer], seq // GRID_W)
        oa = _na_attention(qa, ka, va, kac, vac, bias_tab, batch, seq, n_ctx)
        ob = _swa_attention(qb, kvb, kvbc, sink, batch, seq, n_ctx)
        oc = _fnet(fu, batch, seq)
        xl = _outproj(xl, mod_l, gq, oa, ob, oc, wo, seq)
        xl = _ffn(xl, mod_l, gp, gq, w1b, w2b, 2, seq)

        if not last:
            oac, obc, occ = _ctx_mixer(qac, kac, vac, qbc, kvbc, fuc, sink, batch, n_ctx)
            xc = _outproj(xc, mod_c, gq, oac, obc, occ, wo, n_ctx_tok)
            xc = _ffn(xc, mod_c, gp, gq, w1b, w2b, 2, n_ctx_tok)

    return xl.reshape(batch, seq, D_MODEL)
```

```python
import functools
import math

import numpy as np
import jax
import jax.numpy as jnp
from jax import lax
from jax.experimental import pallas as pl
from jax.experimental.pallas import tpu as pltpu

F32 = jnp.float32
BF16 = jnp.bfloat16

D_MODEL = 1024
GRID_W = 64
HEAD_DIM = 64
NA_HEADS = 6
NA_KR = 8
NA_KW = 16
SWA_HEADS = 6
SWA_KV_HEADS = 2
SWA_WINDOW = 128
SWA_BLOCK = 128
FNET_GROUPS = 4
FNET_GROUP_DIM = 64
NA_WIDTH = NA_HEADS * HEAD_DIM
SWA_WIDTH = SWA_HEADS * HEAD_DIM
SWA_KV_WIDTH = SWA_KV_HEADS * HEAD_DIM
FNET_WIDTH = FNET_GROUPS * FNET_GROUP_DIM
D_IN = 3 * NA_WIDTH + SWA_WIDTH + 2 * SWA_KV_WIDTH + FNET_WIDTH
D_FF = 256 * ((8 * D_MODEL // 3 + 255) // 256)
N_SUB = 3
N_MOD = 3 * N_SUB
MACARON_WEIGHT = 0.5
ROPE_BASE = 10000.0
RMS_EPS = 1e-6
NEG_INF = -1e30
LOG2E = math.log2(math.e)
QK_SCALE_LOG2 = HEAD_DIM ** -0.5 * LOG2E

V7X_LANES = 128
V7X_VMEM_LIMIT_BYTES = 60000 * 1024
MIB = 1024 * 1024

TOKEN_TILE = 512
FFN_CHUNK = 256
MOD_TILE = 1152
NA_SUB_ROWS = 4
NA_SUBS_PER_STEP = 8
SWA_BLOCKS_PER_STEP = 8
NA_KEY_ROWS = NA_SUB_ROWS + NA_KR
FFT_N1 = 16
FFT_N2 = 256

HEADS_PER_BLOCK = V7X_LANES // HEAD_DIM
V_AUG = 2 * V7X_LANES


def _params(vmem_bytes, n_grid):
    limit = int(min(max(vmem_bytes, 16 * MIB), V7X_VMEM_LIMIT_BYTES))
    return pltpu.CompilerParams(dimension_semantics=("arbitrary",) * n_grid, vmem_limit_bytes=limit)


def _rms(x, g):
    return x * lax.rsqrt(jnp.mean(x * x, axis=-1, keepdims=True) + RMS_EPS) * g


def _nt_dot(a, b):
    return lax.dot_general(a, b, (((1,), (1,)), ((), ())), preferred_element_type=F32)


def _dot(a, b):
    return jnp.dot(a, b, preferred_element_type=F32)


def _mod_kernel(c_ref, w_ref, b_ref, o_ref):
    a = c_ref[...]
    act = a / (1.0 + jnp.exp(-a))
    o_ref[0] = _dot(act.astype(BF16), w_ref[0].astype(BF16)) + b_ref[0]


def _modulation(c_rows, w_mod, b_mod):
    depth, _, n_out = w_mod.shape
    rows = c_rows.shape[0]
    return pl.pallas_call(
        _mod_kernel,
        grid=(depth, n_out // MOD_TILE),
        in_specs=[
            pl.BlockSpec((rows, D_MODEL), lambda l, j: (0, 0)),
            pl.BlockSpec((1, D_MODEL, MOD_TILE), lambda l, j: (l, 0, j)),
            pl.BlockSpec((1, 1, MOD_TILE), lambda l, j: (l, 0, j)),
        ],
        out_specs=pl.BlockSpec((1, rows, MOD_TILE), lambda l, j: (l, 0, j)),
        out_shape=jax.ShapeDtypeStruct((depth, rows, n_out), F32),
        compiler_params=_params(4 * D_MODEL * MOD_TILE * 4, 2),
        name="modulation",
    )(c_rows, w_mod, b_mod.reshape(depth, 1, n_out))


def _ffn_kernel(x_ref, mod_ref, gpre_ref, gpost_ref, w1_ref, w2_ref, o_ref, act_ref, *, sub):
    x = x_ref[...]
    shift = mod_ref[0, 3 * sub:3 * sub + 1, :]
    scale = mod_ref[0, 3 * sub + 1:3 * sub + 2, :]
    gate = mod_ref[0, 3 * sub + 2:3 * sub + 3, :]
    h = _rms(x, gpre_ref[sub:sub + 1, :]) * (1.0 + scale) + shift
    hb = h.astype(BF16)
    for c in range(D_FF // FFN_CHUNK):
        lo = c * FFN_CHUNK
        g = _dot(hb, w1_ref[:, lo:lo + FFN_CHUNK])
        u = _dot(hb, w1_ref[:, D_FF + lo:D_FF + lo + FFN_CHUNK])
        act_ref[:, lo:lo + FFN_CHUNK] = (g / (1.0 + jnp.exp(-g)) * u).astype(BF16)
    y = _dot(act_ref[...], w2_ref[...])
    o_ref[...] = x + MACARON_WEIGHT * gate * _rms(y, gpost_ref[sub:sub + 1, :])


def _ffn(x, mod, g_pre, g_post, w1, w2, sub, tokens_per_mod_row):
    n_tok = x.shape[0]
    tiles_per_row = tokens_per_mod_row // TOKEN_TILE
    vmem = (2 * (D_MODEL * 2 * D_FF + D_FF * D_MODEL) * 2 + 4 * TOKEN_TILE * D_MODEL * 4
            + TOKEN_TILE * D_FF * 2 + 8 * MIB)
    return pl.pallas_call(
        functools.partial(_ffn_kernel, sub=sub),
        grid=(n_tok // TOKEN_TILE,),
        in_specs=[
            pl.BlockSpec((TOKEN_TILE, D_MODEL), lambda i: (i, 0)),
            pl.BlockSpec((1, N_MOD, D_MODEL), lambda i: (i // tiles_per_row, 0, 0)),
            pl.BlockSpec((N_SUB, D_MODEL), lambda i: (0, 0)),
            pl.BlockSpec((N_SUB, D_MODEL), lambda i: (0, 0)),
            pl.BlockSpec((D_MODEL, 2 * D_FF), lambda i: (0, 0)),
            pl.BlockSpec((D_FF, D_MODEL), lambda i: (0, 0)),
        ],
        out_specs=pl.BlockSpec((TOKEN_TILE, D_MODEL), lambda i: (i, 0)),
        out_shape=jax.ShapeDtypeStruct((n_tok, D_MODEL), F32),
        scratch_shapes=[pltpu.VMEM((TOKEN_TILE, D_FF), BF16)],
        compiler_params=_params(vmem, 1),
        name="ffn",
    )(x, mod, g_pre, g_post, w1, w2)


_NA_BLOCKS = NA_HEADS // HEADS_PER_BLOCK
_SWA_GROUP = SWA_HEADS // SWA_KV_HEADS


def _half_mask(rows, half):
    lane = lax.broadcasted_iota(jnp.int32, (rows, V7X_LANES), 1)
    return (lane >= HEAD_DIM) if half else (lane < HEAD_DIM)


def _inproj_kernel(*refs, latent):
    if latent:
        (x_ref, mod_ref, gpre_ref, w_ref, cos_ref, sina_ref, sinb_ref,
         qa_ref, ka_ref, va_ref, qb_ref, kb_ref, vb_ref, fu_ref, fu_scr) = refs
    else:
        (x_ref, mod_ref, gpre_ref, w_ref,
         qa_ref, ka_ref, va_ref, qb_ref, kb_ref, vb_ref, fu_ref) = refs
    tile = x_ref.shape[0]
    x = x_ref[...]
    shift = mod_ref[0, 3:4, :]
    scale = mod_ref[0, 4:5, :]
    h = _rms(x, gpre_ref[1:2, :]) * (1.0 + scale) + shift
    r = _dot(h.astype(BF16), w_ref[...])
    ones = jnp.ones((tile, V7X_LANES), BF16)
    o = 0
    qa_ref[...] = (r[:, o:o + NA_WIDTH] * QK_SCALE_LOG2).astype(BF16)
    o += NA_WIDTH
    ka_ref[...] = r[:, o:o + NA_WIDTH].astype(BF16)
    o += NA_WIDTH
    for blk in range(_NA_BLOCKS):
        lo = o + blk * V7X_LANES
        va_ref[:, blk * V_AUG:blk * V_AUG + V7X_LANES] = r[:, lo:lo + V7X_LANES].astype(BF16)
        va_ref[:, blk * V_AUG + V7X_LANES:(blk + 1) * V_AUG] = ones
    o += NA_WIDTH
    qk = r[:, o:o + SWA_WIDTH + SWA_KV_WIDTH]
    o += SWA_WIDTH + SWA_KV_WIDTH
    if latent:
        reps = (SWA_WIDTH + SWA_KV_WIDTH) // V7X_LANES
        width = qk.shape[1]
        cos = jnp.concatenate([cos_ref[...]] * reps, axis=1)
        sina = jnp.concatenate([sina_ref[...]] * reps, axis=1)
        sinb = jnp.concatenate([sinb_ref[...]] * reps, axis=1)
        half = HEAD_DIM // 4
        qk = (qk * cos + pltpu.roll(qk, width - half, axis=1) * sina
              + pltpu.roll(qk, half, axis=1) * sinb)
    for head in range(SWA_HEADS):
        blk = head // HEADS_PER_BLOCK
        piece = qk[:, blk * V7X_LANES:(blk + 1) * V7X_LANES]
        kv_half = head // _SWA_GROUP
        if head % HEADS_PER_BLOCK != kv_half:
            piece = pltpu.roll(piece, HEAD_DIM, axis=1)
        qb_ref[head] = jnp.where(_half_mask(tile, kv_half), piece * QK_SCALE_LOG2, 0.0).astype(BF16)
    kb_ref[...] = qk[:, SWA_WIDTH:].astype(BF16)
    vb_ref[:, 0:V7X_LANES] = r[:, o:o + SWA_KV_WIDTH].astype(BF16)
    vb_ref[:, V7X_LANES:V_AUG] = ones
    o += SWA_KV_WIDTH
    fu = r[:, o:o + FNET_WIDTH]
    if latent:
        rows = tile // FFT_N1
        for blk in range(FNET_WIDTH // V7X_LANES):
            fu_scr[blk] = fu[:, blk * V7X_LANES:(blk + 1) * V7X_LANES]
        for n1 in range(FFT_N1):
            for blk in range(FNET_WIDTH // V7X_LANES):
                lo = n1 * FNET_WIDTH + blk * V7X_LANES
                fu_ref[0, :, lo:lo + V7X_LANES] = (
                    fu_scr[blk, pl.ds(n1, rows, stride=FFT_N1), :].astype(BF16))
    else:
        fu_ref[...] = fu.astype(BF16)


def _rope_tables(seq):
    t = jnp.arange(seq)
    rows = (t // GRID_W).astype(F32)
    cols = (t % GRID_W).astype(F32)
    half = HEAD_DIM // 4
    inv = ROPE_BASE ** (-jnp.arange(half, dtype=F32) / half)
    ang_r = rows[:, None] * inv[None, :]
    ang_c = cols[:, None] * inv[None, :]
    cos_h = jnp.concatenate([jnp.cos(ang_r)] * 2 + [jnp.cos(ang_c)] * 2, axis=1)
    sin_h = jnp.concatenate([jnp.sin(ang_r)] * 2 + [jnp.sin(ang_c)] * 2, axis=1)
    first = (np.arange(HEAD_DIM) % (2 * half)) < half
    sina_h = jnp.where(first[None, :], -sin_h, 0.0)
    sinb_h = jnp.where(first[None, :], 0.0, sin_h)
    two = lambda a: jnp.concatenate([a, a], axis=1)
    return two(cos_h), two(sina_h), two(sinb_h)


def _inproj(x, mod, g_pre, w_in, tokens_per_mod_row, rope_tables):
    n_tok = x.shape[0]
    tiles_per_row = tokens_per_mod_row // TOKEN_TILE
    latent = rope_tables is not None
    in_specs = [
        pl.BlockSpec((TOKEN_TILE, D_MODEL), lambda i: (i, 0)),
        pl.BlockSpec((1, N_MOD, D_MODEL), lambda i: (i // tiles_per_row, 0, 0)),
        pl.BlockSpec((N_SUB, D_MODEL), lambda i: (0, 0)),
        pl.BlockSpec((D_MODEL, D_IN), lambda i: (0, 0)),
    ]
    args = [x, mod, g_pre, w_in]
    tok = lambda w: (pl.BlockSpec((TOKEN_TILE, w), lambda i: (i, 0)), jax.ShapeDtypeStruct((n_tok, w), BF16))
    outs = [tok(NA_WIDTH), tok(NA_WIDTH), tok(_NA_BLOCKS * V_AUG),
            (pl.BlockSpec((SWA_HEADS, TOKEN_TILE, V7X_LANES), lambda i: (0, i, 0)),
             jax.ShapeDtypeStruct((SWA_HEADS, n_tok, V7X_LANES), BF16)),
            tok(SWA_KV_WIDTH), tok(V_AUG)]
    scratch = []
    if latent:
        for tab in rope_tables:
            in_specs.append(pl.BlockSpec((TOKEN_TILE, V7X_LANES), lambda i: (i % tiles_per_row, 0)))
            args.append(tab)
        rows = TOKEN_TILE // FFT_N1
        outs.append((pl.BlockSpec((1, rows, FFT_N1 * FNET_WIDTH),
                                  lambda i: (i // tiles_per_row, i % tiles_per_row, 0)),
                     jax.ShapeDtypeStruct((n_tok // tokens_per_mod_row, tokens_per_mod_row // FFT_N1,
                                           FFT_N1 * FNET_WIDTH), BF16)))
        scratch.append(pltpu.VMEM((FNET_WIDTH // V7X_LANES, TOKEN_TILE, V7X_LANES), F32))
    else:
        outs.append(tok(FNET_WIDTH))
    vmem = 2 * D_MODEL * D_IN * 2 + 4 * TOKEN_TILE * D_MODEL * 4 + 6 * TOKEN_TILE * D_IN * 4 + 8 * MIB
    return pl.pallas_call(
        functools.partial(_inproj_kernel, latent=latent),
        grid=(n_tok // TOKEN_TILE,),
        in_specs=in_specs,
        out_specs=[s for s, _ in outs],
        out_shape=[s for _, s in outs],
        scratch_shapes=scratch,
        compiler_params=_params(vmem, 1),
        name="inproj_latent" if latent else "inproj_ctx",
    )(*args)


def _stack_heads(q):
    rows = q.shape[0]
    zero = jnp.zeros_like(q)
    return jnp.concatenate([jnp.where(_half_mask(rows, 0), q, zero),
                            jnp.where(_half_mask(rows, 1), q, zero)], axis=0)


def _attend(q, keys, values, biases, extra_logit=None):
    scores = []
    for k, b in zip(keys, biases):
        s = _nt_dot(q, k)
        scores.append(s if b is None else s + b)
    m = scores[0].max(axis=-1, keepdims=True)
    for s in scores[1:]:
        m = jnp.maximum(m, s.max(axis=-1, keepdims=True))
    if extra_logit is not None:
        m = jnp.maximum(m, extra_logit)
    acc = None
    for s, v in zip(scores, values):
        pv = _dot(jnp.exp2(s - m).astype(BF16), v)
        acc = pv if acc is None else acc + pv
    num = acc[:, :V7X_LANES]
    den = acc[:, V7X_LANES:]
    if extra_logit is not None:
        den = den + jnp.exp2(extra_logit - m)
    return num / den


_NA_Q = NA_SUB_ROWS * GRID_W
_NA_K = NA_KEY_ROWS * GRID_W
_NA_M = HEADS_PER_BLOCK * _NA_Q


def _na_kernel(q_ref, k_ref, v_ref, kc_ref, vc_ref, bias_ref, o_ref, *, grid_rows):
    j = pl.program_id(2)
    n_slabs = grid_rows // NA_SUB_ROWS
    kc = kc_ref[...]
    vc = vc_ref[...]
    for sub in range(NA_SUBS_PER_STEP):
        slab = j * NA_SUBS_PER_STEP + sub
        case = jnp.where(slab == 0, 0, jnp.where(slab == n_slabs - 1, 2, 1))
        start_row = jnp.clip(slab * NA_SUB_ROWS - NA_KR // 2, 0, grid_rows - NA_KEY_ROWS)
        start = pl.multiple_of(start_row * GRID_W, NA_SUB_ROWS * GRID_W)
        q2 = q_ref[sub * _NA_Q:(sub + 1) * _NA_Q, :]
        k = k_ref[pl.ds(start, _NA_K), :]
        v = v_ref[pl.ds(start, _NA_K), :]
        res = []
        for half in range(HEADS_PER_BLOCK):
            q = jnp.where(_half_mask(_NA_Q, half), q2, jnp.zeros_like(q2))
            bias = bias_ref[case, 0, half * _NA_Q:(half + 1) * _NA_Q, :]
            res.append(_attend(q, (k, kc), (v, vc), (bias, None)))
        o_ref[sub * _NA_Q:(sub + 1) * _NA_Q, :] = jnp.where(
            _half_mask(_NA_Q, 0), res[0], res[1]).astype(BF16)


def _na_window_tables(grid_rows):
    n_subs = grid_rows // NA_SUB_ROWS
    kr = min(NA_KR, grid_rows)
    ro = np.zeros((3, NA_SUB_ROWS, NA_KEY_ROWS), np.int32)
    rvalid = np.zeros(ro.shape, bool)
    for ci, u in enumerate((0, min(1, n_subs - 1), n_subs - 1)):
        start_row = int(np.clip(u * NA_SUB_ROWS - NA_KR // 2, 0, grid_rows - NA_KEY_ROWS))
        for l in range(NA_SUB_ROWS):
            rq = u * NA_SUB_ROWS + l
            rs = int(np.clip(rq - kr // 2, 0, grid_rows - kr))
            for i in range(NA_KEY_ROWS):
                rk = start_row + i
                rvalid[ci, l, i] = rs <= rk < rs + kr
                ro[ci, l, i] = int(np.clip(rk - rq + NA_KR - 1, 0, 2 * NA_KR - 2))
    return ro, rvalid


def _na_bias_kernel(rpb_ref, o_ref, toep_ref, *, ro, rvalid, n_dr, n_dc):
    p = pl.program_id(0)
    cq = lax.broadcasted_iota(jnp.int32, (GRID_W, GRID_W), 0)
    ck = lax.broadcasted_iota(jnp.int32, (GRID_W, GRID_W), 1)
    ws = jnp.clip(cq - NA_KW // 2, 0, GRID_W - NA_KW)
    cvalid = (ck >= ws) & (ck < ws + NA_KW)
    co = ck - cq + NA_KW - 1
    neg = jnp.full((GRID_W, GRID_W), NEG_INF, F32)
    n_cases, n_l, n_i = ro.shape
    for hh in range(HEADS_PER_BLOCK):
        head = p * HEADS_PER_BLOCK + hh
        for dr in range(n_dr):
            tile = neg
            for dc in range(n_dc):
                tile = jnp.where(co == dc, rpb_ref[(head * n_dr + dr) * n_dc + dc] * LOG2E, tile)
            toep_ref[dr] = jnp.where(cvalid, tile, neg)
        for c in range(n_cases):
            for l in range(n_l):
                r0 = (hh * n_l + l) * GRID_W
                for i in range(n_i):
                    tile = toep_ref[int(ro[c, l, i])] if rvalid[c, l, i] else neg
                    o_ref[c, 0, r0:r0 + GRID_W, i * GRID_W:(i + 1) * GRID_W] = tile


def _na_bias_tables(rpb, grid_rows):
    ro, rvalid = _na_window_tables(grid_rows)
    n_dr, n_dc = rpb.shape[1], rpb.shape[2]
    return pl.pallas_call(
        functools.partial(_na_bias_kernel, ro=ro, rvalid=rvalid, n_dr=n_dr, n_dc=n_dc),
        grid=(_NA_BLOCKS,),
        in_specs=[pl.BlockSpec(memory_space=pltpu.SMEM)],
        out_specs=pl.BlockSpec((3, 1, _NA_M, _NA_K), lambda p: (0, p, 0, 0)),
        out_shape=jax.ShapeDtypeStruct((3, _NA_BLOCKS, _NA_M, _NA_K), F32),
        scratch_shapes=[pltpu.VMEM((n_dr, GRID_W, GRID_W), F32)],
        compiler_params=_params(4 * 3 * _NA_M * _NA_K * 4, 1),
        name="na_bias",
    )(rpb.astype(F32).reshape(-1))


def _na_attention(qa, ka, va, kac, vac, bias_tab, batch, seq, n_ctx):
    grid_rows = seq // GRID_W
    rows_q = NA_SUBS_PER_STEP * _NA_Q
    n_steps = seq // rows_q
    bias_block = (3, 1, _NA_M, _NA_K)
    vmem = (2 * 3 * _NA_M * _NA_K * 4 + 2 * seq * (V7X_LANES + V_AUG) * 2 + 16 * MIB)
    return pl.pallas_call(
        functools.partial(_na_kernel, grid_rows=grid_rows),
        grid=(batch, _NA_BLOCKS, n_steps),
        in_specs=[
            pl.BlockSpec((rows_q, V7X_LANES), lambda b, p, j: (b * n_steps + j, p)),
            pl.BlockSpec((seq, V7X_LANES), lambda b, p, j: (b, p)),
            pl.BlockSpec((seq, V_AUG), lambda b, p, j: (b, p)),
            pl.BlockSpec((n_ctx, V7X_LANES), lambda b, p, j: (b, p)),
            pl.BlockSpec((n_ctx, V_AUG), lambda b, p, j: (b, p)),
            pl.BlockSpec(bias_block, lambda b, p, j: (0, p, 0, 0)),
        ],
        out_specs=pl.BlockSpec((rows_q, V7X_LANES), lambda b, p, j: (b * n_steps + j, p)),
        out_shape=jax.ShapeDtypeStruct((batch * seq, NA_WIDTH), BF16),
        compiler_params=_params(vmem, 3),
        name="na_attention",
    )(qa, ka, va, kac, vac, bias_tab)


_SWA_SPAN = 3 * SWA_BLOCK


def _gather_swa_heads(res, rows):
    blocks = []
    for blk in range(SWA_HEADS // HEADS_PER_BLOCK):
        halves = []
        for half in range(HEADS_PER_BLOCK):
            head = blk * HEADS_PER_BLOCK + half
            piece = res[head]
            if head // _SWA_GROUP != half:
                piece = pltpu.roll(piece, HEAD_DIM, axis=1)
            halves.append(piece)
        blocks.append(jnp.where(_half_mask(rows, 0), halves[0], halves[1]))
    return jnp.concatenate(blocks, axis=1)


def _swa_kernel(sink_ref, q_ref, k_ref, v_ref, kc_ref, vc_ref, mask_ref, o_ref, *, seq):
    n = pl.program_id(1)
    nb = seq // SWA_BLOCK
    kc = kc_ref[...]
    vc = vc_ref[...]
    for sub in range(SWA_BLOCKS_PER_STEP):
        blk = n * SWA_BLOCKS_PER_STEP + sub
        case = jnp.where(blk == 0, 0, jnp.where(blk == nb - 1, 2, 1))
        start = pl.multiple_of(jnp.clip((blk - 1) * SWA_BLOCK, 0, seq - _SWA_SPAN), SWA_BLOCK)
        k = k_ref[pl.ds(start, _SWA_SPAN), :]
        v = v_ref[pl.ds(start, _SWA_SPAN), :]
        mask = mask_ref[case]
        rows = slice(sub * SWA_BLOCK, (sub + 1) * SWA_BLOCK)
        res = []
        for head in range(SWA_HEADS):
            sink = jnp.full((SWA_BLOCK, 1), sink_ref[head] * LOG2E, F32)
            res.append(_attend(q_ref[head, rows, :], (k, kc), (v, vc), (mask, None), extra_logit=sink))
        o_ref[rows, :] = _gather_swa_heads(res, SWA_BLOCK).astype(BF16)


def _swa_masks(seq):
    nb = seq // SWA_BLOCK
    out = np.zeros((3, SWA_BLOCK, _SWA_SPAN), np.float32)
    for ci, n in enumerate((0, min(1, nb - 1), nb - 1)):
        start = int(np.clip((n - 1) * SWA_BLOCK, 0, seq - _SWA_SPAN))
        kpos = start + np.arange(_SWA_SPAN)[None, :]
        qpos = n * SWA_BLOCK + np.arange(SWA_BLOCK)[:, None]
        out[ci] = np.where(np.abs(kpos - qpos) <= SWA_WINDOW, 0.0, NEG_INF)
    return jnp.asarray(out)


def _swa_attention(qb, kb, vb, kbc, vbc, sink, batch, seq, n_ctx):
    rows = SWA_BLOCKS_PER_STEP * SWA_BLOCK
    ns = seq // rows
    vmem = 4 * seq * (SWA_KV_WIDTH + V_AUG) * 2 + 16 * MIB
    return pl.pallas_call(
        functools.partial(_swa_kernel, seq=seq),
        grid=(batch, ns),
        in_specs=[
            pl.BlockSpec(memory_space=pltpu.SMEM),
            pl.BlockSpec((SWA_HEADS, rows, V7X_LANES), lambda b, n: (0, b * ns + n, 0)),
            pl.BlockSpec((seq, SWA_KV_WIDTH), lambda b, n: (b, 0)),
            pl.BlockSpec((seq, V_AUG), lambda b, n: (b, 0)),
            pl.BlockSpec((n_ctx, SWA_KV_WIDTH), lambda b, n: (b, 0)),
            pl.BlockSpec((n_ctx, V_AUG), lambda b, n: (b, 0)),
            pl.BlockSpec((3, SWA_BLOCK, _SWA_SPAN), lambda b, n: (0, 0, 0)),
        ],
        out_specs=pl.BlockSpec((rows, SWA_WIDTH), lambda b, n: (b * ns + n, 0)),
        out_shape=jax.ShapeDtypeStruct((batch * seq, SWA_WIDTH), BF16),
        compiler_params=_params(vmem, 2),
        name="swa_attention",
    )(sink, qb, kb, vb, kbc, vbc, _swa_masks(seq))


def _dft_constants():
    jc = np.arange(FNET_GROUP_DIM)
    ang_c = 2.0 * np.pi * ((jc[:, None] * jc[None, :]) % FNET_GROUP_DIM) / FNET_GROUP_DIM
    eye = np.eye(FNET_GROUPS)
    bd_cos = np.kron(eye, np.cos(ang_c))
    bd_sin = np.kron(eye, np.sin(ang_c))
    chan = np.concatenate([bd_cos, -bd_sin], axis=1)
    jp = np.arange(FFT_N2)
    ang_p = 2.0 * np.pi * ((jp[:, None] * jp[None, :]) % FFT_N2) / FFT_N2
    cp, sp = np.cos(ang_p), np.sin(ang_p)
    pos = np.block([[cp, sp], [-sp, cp]])
    return jnp.asarray(chan, F32), jnp.asarray(pos, F32)


def _twiddle_tables(n_total):
    n1 = np.arange(FFT_N1)[:, None]
    k2 = np.arange(FFT_N2)[None, :]
    ang = 2.0 * np.pi * ((n1 * k2) % n_total) / n_total
    bc = lambda a: jnp.asarray(np.repeat(a[:, :, None], V7X_LANES, axis=2), F32)
    return bc(np.cos(ang)), bc(np.sin(ang))


def _dft256_of_channels(u, chan, pos):
    z = _dot(u, chan)
    zs = jnp.concatenate([z[:, :FNET_WIDTH], z[:, FNET_WIDTH:]], axis=0).astype(BF16)
    a = _dot(pos, zs)
    return a[:FFT_N2], a[FFT_N2:]


def _fnet_kernel(u_ref, chan_ref, pos_ref, twc_ref, tws_ref, o_ref, bre_ref, bim_ref, *, scale):
    chan = chan_ref[...].astype(BF16)
    pos = pos_ref[...].astype(BF16)
    for n1 in range(FFT_N1):
        u = u_ref[0, :, n1 * FNET_WIDTH:(n1 + 1) * FNET_WIDTH]
        a_re, a_im = _dft256_of_channels(u, chan, pos)
        if n1 == 0:
            bre_ref[0] = a_re
            bim_ref[0] = a_im
        else:
            c = jnp.concatenate([twc_ref[n1]] * (FNET_WIDTH // V7X_LANES), axis=1)
            s = jnp.concatenate([tws_ref[n1]] * (FNET_WIDTH // V7X_LANES), axis=1)
            bre_ref[n1] = a_re * c + a_im * s
            bim_ref[n1] = a_im * c - a_re * s
    for k1 in range(FFT_N1):
        acc = bre_ref[0]
        for n1 in range(1, FFT_N1):
            m = (n1 * k1) % FFT_N1
            c = float(np.cos(2.0 * np.pi * m / FFT_N1))
            s = float(np.sin(2.0 * np.pi * m / FFT_N1))
            if abs(c) > 1e-9:
                acc = acc + c * bre_ref[n1]
            if abs(s) > 1e-9:
                acc = acc + s * bim_ref[n1]
        o_ref[k1 * FFT_N2:(k1 + 1) * FFT_N2, :] = (acc * scale).astype(BF16)


def _fnet(fu, batch, seq):
    assert seq == FFT_N1 * FFT_N2
    assert fu.shape == (batch, FFT_N2, FFT_N1 * FNET_WIDTH)
    chan, pos = _dft_constants()
    twc, tws = _twiddle_tables(seq)
    scale = float((seq * FNET_GROUP_DIM) ** -0.5)
    vmem = 2 * FFT_N1 * FFT_N2 * FNET_WIDTH * 4 + 8 * seq * FNET_WIDTH * 2 + 16 * MIB
    return pl.pallas_call(
        functools.partial(_fnet_kernel, scale=scale),
        grid=(batch,),
        in_specs=[
            pl.BlockSpec((1, FFT_N2, FFT_N1 * FNET_WIDTH), lambda b: (b, 0, 0)),
            pl.BlockSpec(chan.shape, lambda b: (0, 0)),
            pl.BlockSpec(pos.shape, lambda b: (0, 0)),
            pl.BlockSpec(twc.shape, lambda b: (0, 0, 0)),
            pl.BlockSpec(tws.shape, lambda b: (0, 0, 0)),
        ],
        out_specs=pl.BlockSpec((seq, FNET_WIDTH), lambda b: (b, 0)),
        out_shape=jax.ShapeDtypeStruct((batch * seq, FNET_WIDTH), BF16),
        scratch_shapes=[pltpu.VMEM((FFT_N1, FFT_N2, FNET_WIDTH), F32),
                        pltpu.VMEM((FFT_N1, FFT_N2, FNET_WIDTH), F32)],
        compiler_params=_params(vmem, 1),
        name="fnet",
    )(fu, chan, pos, twc, tws)


def _ctx_mixer_kernel(sink_ref, qa_ref, ka_ref, va_ref, qb_ref, kb_ref, vb_ref, fu_ref, chan_ref,
                      pos_ref, oa_ref, ob_ref, oc_ref, *, n_ctx, scale):
    for blk in range(_NA_BLOCKS):
        sl = slice(blk * V7X_LANES, (blk + 1) * V7X_LANES)
        q = _stack_heads(qa_ref[:, sl])
        res = _attend(q, (ka_ref[:, sl],), (va_ref[:, blk * V_AUG:(blk + 1) * V_AUG],), (None,))
        oa_ref[:, sl] = jnp.where(_half_mask(n_ctx, 0), res[:n_ctx], res[n_ctx:]).astype(BF16)
    k = kb_ref[...]
    v = vb_ref[...]
    res = []
    for head in range(SWA_HEADS):
        sink = jnp.full((n_ctx, 1), sink_ref[head] * LOG2E, F32)
        res.append(_attend(qb_ref[head], (k,), (v,), (None,), extra_logit=sink))
    ob_ref[...] = _gather_swa_heads(res, n_ctx).astype(BF16)
    a_re, _ = _dft256_of_channels(fu_ref[...], chan_ref[...].astype(BF16), pos_ref[...].astype(BF16))
    oc_ref[...] = (a_re * scale).astype(BF16)


def _ctx_mixer(qa, ka, va, qb, kb, vb, fu, sink, batch, n_ctx):
    assert n_ctx == FFT_N2
    chan, pos = _dft_constants()
    scale = float((n_ctx * FNET_GROUP_DIM) ** -0.5)
    tok = lambda w: pl.BlockSpec((n_ctx, w), lambda b: (b, 0))
    widths = (NA_WIDTH, SWA_WIDTH, FNET_WIDTH)
    return pl.pallas_call(
        functools.partial(_ctx_mixer_kernel, n_ctx=n_ctx, scale=scale),
        grid=(batch,),
        in_specs=[
            pl.BlockSpec(memory_space=pltpu.SMEM),
            tok(NA_WIDTH), tok(NA_WIDTH), tok(_NA_BLOCKS * V_AUG),
            pl.BlockSpec((SWA_HEADS, n_ctx, V7X_LANES), lambda b: (0, b, 0)),
            tok(SWA_KV_WIDTH), tok(V_AUG), tok(FNET_WIDTH),
            pl.BlockSpec(chan.shape, lambda b: (0, 0)),
            pl.BlockSpec(pos.shape, lambda b: (0, 0)),
        ],
        out_specs=[tok(w) for w in widths],
        out_shape=[jax.ShapeDtypeStruct((batch * n_ctx, w), BF16) for w in widths],
        compiler_params=_params(24 * MIB, 1),
        name="ctx_mixer",
    )(sink, qa, ka, va, qb, kb, vb, fu, chan, pos)


def _outproj_kernel(x_ref, mod_ref, gpost_ref, oa_ref, ob_ref, oc_ref, w_ref, o_ref):
    gate = mod_ref[0, 5:6, :]
    y = _dot(oa_ref[...], w_ref[0:NA_WIDTH, :])
    y = y + _dot(ob_ref[...], w_ref[NA_WIDTH:NA_WIDTH + SWA_WIDTH, :])
    y = y + _dot(oc_ref[...], w_ref[NA_WIDTH + SWA_WIDTH:, :])
    o_ref[...] = x_ref[...] + gate * _rms(y, gpost_ref[1:2, :])


def _outproj(x, mod, g_post, oa, ob, oc, w_out, tokens_per_mod_row):
    n_tok = x.shape[0]
    tiles_per_row = tokens_per_mod_row // TOKEN_TILE
    tok = lambda w: pl.BlockSpec((TOKEN_TILE, w), lambda i: (i, 0))
    vmem = 2 * D_MODEL * D_MODEL * 2 + 8 * TOKEN_TILE * D_MODEL * 4 + 8 * MIB
    return pl.pallas_call(
        _outproj_kernel,
        grid=(n_tok // TOKEN_TILE,),
        in_specs=[
            tok(D_MODEL),
            pl.BlockSpec((1, N_MOD, D_MODEL), lambda i: (i // tiles_per_row, 0, 0)),
            pl.BlockSpec((N_SUB, D_MODEL), lambda i: (0, 0)),
            tok(NA_WIDTH), tok(SWA_WIDTH), tok(FNET_WIDTH),
            pl.BlockSpec((D_MODEL, D_MODEL), lambda i: (0, 0)),
        ],
        out_specs=tok(D_MODEL),
        out_shape=jax.ShapeDtypeStruct((n_tok, D_MODEL), F32),
        compiler_params=_params(vmem, 1),
        name="outproj",
    )(x, mod, g_post, oa, ob, oc, w_out)


def kernel(x, c, ctx, c_ctx, w_mod, b_mod, g_pre, g_post, w_ffn_in, w_ffn_out, w_in, w_out, na_rpb, swa_sink):
    batch, seq, _ = x.shape
    n_ctx = ctx.shape[1]
    depth = w_mod.shape[0]
    assert seq % (NA_SUBS_PER_STEP * NA_SUB_ROWS * GRID_W) == 0 and seq % TOKEN_TILE == 0
    assert (batch * n_ctx) % TOKEN_TILE == 0

    mod_rows = -(-(batch + 1) // 8) * 8
    c_rows = jnp.concatenate(
        [c, c_ctx[None, :], jnp.zeros((mod_rows - batch - 1, D_MODEL), F32)], axis=0)
    mod_all = _modulation(c_rows, w_mod, b_mod).reshape(depth, mod_rows, N_MOD, D_MODEL)

    rope_tables = _rope_tables(seq)
    xl = x.reshape(batch * seq, D_MODEL)
    xc = ctx.reshape(batch * n_ctx, D_MODEL)
    n_ctx_tok = batch * n_ctx

    for layer in range(depth):
        mod_l = mod_all[layer, :batch]
        mod_c = mod_all[layer, batch:batch + 1]
        gp, gq = g_pre[layer], g_post[layer]
        w1a, w1b = w_ffn_in[layer, 0].astype(BF16), w_ffn_in[layer, 1].astype(BF16)
        w2a, w2b = w_ffn_out[layer, 0].astype(BF16), w_ffn_out[layer, 1].astype(BF16)
        wi, wo = w_in[layer].astype(BF16), w_out[layer].astype(BF16)
        sink = swa_sink[layer].astype(F32)
        last = layer == depth - 1

        xl = _ffn(xl, mod_l, gp, gq, w1a, w2a, 0, seq)
        xc = _ffn(xc, mod_c, gp, gq, w1a, w2a, 0, n_ctx_tok)

        qa, ka, va, qb, kb, vb, fu = _inproj(xl, mod_l, gp, wi, seq, rope_tables)
        qac, kac, vac, qbc, kbc, vbc, fuc = _inproj(xc, mod_c, gp, wi, n_ctx_tok, None)

        bias_tab = _na_bias_tables(na_rpb[layer], seq // GRID_W)
        oa = _na_attention(qa, ka, va, kac, vac, bias_tab, batch, seq, n_ctx)
        ob = _swa_attention(qb, kb, vb, kbc, vbc, sink, batch, seq, n_ctx)
        oc = _fnet(fu, batch, seq)
        xl = _outproj(xl, mod_l, gq, oa, ob, oc, wo, seq)
        xl = _ffn(xl, mod_l, gp, gq, w1b, w2b, 2, seq)

        if not last:
            oac, obc, occ = _ctx_mixer(qac, kac, vac, qbc, kbc, vbc, fuc, sink, batch, n_ctx)
            xc = _outproj(xc, mod_c, gq, oac, obc, occ, wo, n_ctx_tok)
            xc = _ffn(xc, mod_c, gp, gq, w1b, w2b, 2, n_ctx_tok)

    return xl.reshape(batch, seq, D_MODEL)
```

```python
import functools
import math

import numpy as np
import jax
import jax.numpy as jnp
from jax import lax
from jax.experimental import pallas as pl
from jax.experimental.pallas import tpu as pltpu

F32 = jnp.float32
BF16 = jnp.bfloat16

D_MODEL = 1024
GRID_W = 64
HEAD_DIM = 64
NA_HEADS = 6
NA_KR = 8
NA_KW = 16
SWA_HEADS = 6
SWA_KV_HEADS = 2
SWA_WINDOW = 128
SWA_BLOCK = 128
FNET_GROUPS = 4
FNET_GROUP_DIM = 64
NA_WIDTH = NA_HEADS * HEAD_DIM
SWA_WIDTH = SWA_HEADS * HEAD_DIM
SWA_KV_WIDTH = SWA_KV_HEADS * HEAD_DIM
FNET_WIDTH = FNET_GROUPS * FNET_GROUP_DIM
D_IN = 3 * NA_WIDTH + SWA_WIDTH + 2 * SWA_KV_WIDTH + FNET_WIDTH
D_FF = 256 * ((8 * D_MODEL // 3 + 255) // 256)
N_SUB = 3
N_MOD = 3 * N_SUB
MACARON_WEIGHT = 0.5
ROPE_BASE = 10000.0
RMS_EPS = 1e-6
NEG_INF = -1e30
LOG2E = math.log2(math.e)
QK_SCALE_LOG2 = HEAD_DIM ** -0.5 * LOG2E

V7X_LANES = 128
V7X_VMEM_LIMIT_BYTES = 60000 * 1024
MIB = 1024 * 1024

TOKEN_TILE = 1024
TOKEN_SUB = 512
FFN_CHUNK = 256
MOD_TILE = 1152
NA_SUB_ROWS = 4
NA_SUBS_PER_STEP = 8
SWA_BLOCKS_PER_STEP = 8
NA_KEY_ROWS = NA_SUB_ROWS + NA_KR
FFT_N1 = 16
FFT_N2 = 256

HEADS_PER_BLOCK = V7X_LANES // HEAD_DIM
V_AUG = 2 * V7X_LANES


def _params(vmem_bytes, n_grid):
    limit = int(min(max(vmem_bytes, 16 * MIB), V7X_VMEM_LIMIT_BYTES))
    return pltpu.CompilerParams(dimension_semantics=("arbitrary",) * n_grid, vmem_limit_bytes=limit)


def _rms(x, g):
    return x * lax.rsqrt(jnp.mean(x * x, axis=-1, keepdims=True) + RMS_EPS) * g


def _nt_dot(a, b):
    return lax.dot_general(a, b, (((1,), (1,)), ((), ())), preferred_element_type=F32)


def _dot(a, b):
    return jnp.dot(a, b, preferred_element_type=F32)


def _mod_kernel(c_ref, w_ref, b_ref, o_ref):
    a = c_ref[...]
    act = a / (1.0 + jnp.exp(-a))
    o_ref[0] = _dot(act.astype(BF16), w_ref[0].astype(BF16)) + b_ref[0]


def _modulation(c_rows, w_mod, b_mod):
    depth, _, n_out = w_mod.shape
    rows = c_rows.shape[0]
    return pl.pallas_call(
        _mod_kernel,
        grid=(depth, n_out // MOD_TILE),
        in_specs=[
            pl.BlockSpec((rows, D_MODEL), lambda l, j: (0, 0)),
            pl.BlockSpec((1, D_MODEL, MOD_TILE), lambda l, j: (l, 0, j)),
            pl.BlockSpec((1, 1, MOD_TILE), lambda l, j: (l, 0, j)),
        ],
        out_specs=pl.BlockSpec((1, rows, MOD_TILE), lambda l, j: (l, 0, j)),
        out_shape=jax.ShapeDtypeStruct((depth, rows, n_out), F32),
        compiler_params=_params(4 * D_MODEL * MOD_TILE * 4, 2),
        name="modulation",
    )(c_rows, w_mod, b_mod.reshape(depth, 1, n_out))


def _token_specs(layer, mod_row):
    tok = lambda w: pl.BlockSpec((TOKEN_TILE, w), lambda i: (i, 0))
    mod = pl.BlockSpec((None, 1, N_MOD, D_MODEL), lambda i: (layer, mod_row(i), 0, 0))
    gain = pl.BlockSpec((None, N_SUB, D_MODEL), lambda i: (layer, 0, 0))
    return tok, mod, gain


def _resident(shape, index):
    return pl.BlockSpec(shape, lambda i: index, pipeline_mode=pl.Buffered(1))


def _ffn_kernel(*refs, sub, mix):
    if mix:
        (x_ref, mod_ref, gpre_ref, gpost_ref, oa_ref, ob_ref, oc_ref, wo_ref, w1_ref, w2_ref,
         o_ref, act_ref) = refs
    else:
        x_ref, mod_ref, gpre_ref, gpost_ref, w1_ref, w2_ref, o_ref, act_ref = refs
    shift = mod_ref[0, 3 * sub:3 * sub + 1, :]
    scale = mod_ref[0, 3 * sub + 1:3 * sub + 2, :]
    gate = mod_ref[0, 3 * sub + 2:3 * sub + 3, :]
    for part in range(TOKEN_TILE // TOKEN_SUB):
        rows = slice(part * TOKEN_SUB, (part + 1) * TOKEN_SUB)
        x = x_ref[rows, :]
        if mix:
            y = _dot(oa_ref[rows, :], wo_ref[0:NA_WIDTH, :])
            y = y + _dot(ob_ref[rows, :], wo_ref[NA_WIDTH:NA_WIDTH + SWA_WIDTH, :])
            y = y + _dot(oc_ref[rows, :], wo_ref[NA_WIDTH + SWA_WIDTH:, :])
            x = x + mod_ref[0, 5:6, :] * _rms(y, gpost_ref[1:2, :])
        h = _rms(x, gpre_ref[sub:sub + 1, :]) * (1.0 + scale) + shift
        hb = h.astype(BF16)
        for c in range(D_FF // FFN_CHUNK):
            lo = c * FFN_CHUNK
            g = _dot(hb, w1_ref[:, lo:lo + FFN_CHUNK])
            u = _dot(hb, w1_ref[:, D_FF + lo:D_FF + lo + FFN_CHUNK])
            act_ref[part, :, lo:lo + FFN_CHUNK] = (g / (1.0 + jnp.exp(-g)) * u).astype(BF16)
        y = _dot(act_ref[part], w2_ref[...])
        o_ref[rows, :] = x + MACARON_WEIGHT * gate * _rms(y, gpost_ref[sub:sub + 1, :])


def _ffn(x, mod_all, g_pre, g_post, w1_all, w2_all, layer, which, mod_row, mix=None):
    n_tok = x.shape[0]
    sub = 2 * which
    tok, mod_spec, gain_spec = _token_specs(layer, mod_row)
    in_specs = [tok(D_MODEL), mod_spec, gain_spec, gain_spec]
    args = [x, mod_all, g_pre, g_post]
    vmem = ((D_MODEL * 2 * D_FF + D_FF * D_MODEL) * 2 + 4 * TOKEN_TILE * D_MODEL * 4
            + TOKEN_TILE * D_FF * 2 + 12 * MIB)
    if mix is not None:
        oa, ob, oc, w_out_all = mix
        in_specs += [tok(NA_WIDTH), tok(SWA_WIDTH), tok(FNET_WIDTH),
                     _resident((None, D_MODEL, D_MODEL), (layer, 0, 0))]
        args += [oa, ob, oc, w_out_all]
        vmem += D_MODEL * D_MODEL * 2 + 2 * TOKEN_TILE * D_MODEL * 2
    in_specs += [_resident((None, None, D_MODEL, 2 * D_FF), (layer, which, 0, 0)),
                 _resident((None, None, D_FF, D_MODEL), (layer, which, 0, 0))]
    args += [w1_all, w2_all]
    return pl.pallas_call(
        functools.partial(_ffn_kernel, sub=sub, mix=mix is not None),
        grid=(n_tok // TOKEN_TILE,),
        in_specs=in_specs,
        out_specs=tok(D_MODEL),
        out_shape=jax.ShapeDtypeStruct((n_tok, D_MODEL), F32),
        scratch_shapes=[pltpu.VMEM((TOKEN_TILE // TOKEN_SUB, TOKEN_SUB, D_FF), BF16)],
        compiler_params=_params(vmem, 1),
        name="mix_ffn" if mix is not None else "ffn",
    )(*args)


_NA_BLOCKS = NA_HEADS // HEADS_PER_BLOCK
_SWA_GROUP = SWA_HEADS // SWA_KV_HEADS


def _half_mask(rows, half):
    lane = lax.broadcasted_iota(jnp.int32, (rows, V7X_LANES), 1)
    return (lane >= HEAD_DIM) if half else (lane < HEAD_DIM)


def _inproj_kernel(*refs, latent):
    if latent:
        (x_ref, mod_ref, gpre_ref, w_ref, cos_ref, sina_ref, sinb_ref,
         qa_ref, ka_ref, va_ref, qb_ref, kb_ref, vb_ref, fu_ref, fu_scr) = refs
    else:
        (x_ref, mod_ref, gpre_ref, w_ref,
         qa_ref, ka_ref, va_ref, qb_ref, kb_ref, vb_ref, fu_ref) = refs
    shift = mod_ref[0, 3:4, :]
    scale = mod_ref[0, 4:5, :]
    ones = jnp.ones((TOKEN_SUB, V7X_LANES), BF16)
    for part in range(TOKEN_TILE // TOKEN_SUB):
        rows = slice(part * TOKEN_SUB, (part + 1) * TOKEN_SUB)
        h = _rms(x_ref[rows, :], gpre_ref[1:2, :]) * (1.0 + scale) + shift
        r = _dot(h.astype(BF16), w_ref[...])
        o = 0
        qa_ref[rows, :] = (r[:, o:o + NA_WIDTH] * QK_SCALE_LOG2).astype(BF16)
        o += NA_WIDTH
        ka_ref[rows, :] = r[:, o:o + NA_WIDTH].astype(BF16)
        o += NA_WIDTH
        for blk in range(_NA_BLOCKS):
            lo = o + blk * V7X_LANES
            va_ref[rows, blk * V_AUG:blk * V_AUG + V7X_LANES] = r[:, lo:lo + V7X_LANES].astype(BF16)
            va_ref[rows, blk * V_AUG + V7X_LANES:(blk + 1) * V_AUG] = ones
        o += NA_WIDTH
        qk = r[:, o:o + SWA_WIDTH + SWA_KV_WIDTH]
        o += SWA_WIDTH + SWA_KV_WIDTH
        if latent:
            reps = (SWA_WIDTH + SWA_KV_WIDTH) // V7X_LANES
            width = qk.shape[1]
            cos = jnp.concatenate([cos_ref[rows, :]] * reps, axis=1)
            sina = jnp.concatenate([sina_ref[rows, :]] * reps, axis=1)
            sinb = jnp.concatenate([sinb_ref[rows, :]] * reps, axis=1)
            half = HEAD_DIM // 4
            qk = (qk * cos + pltpu.roll(qk, width - half, axis=1) * sina
                  + pltpu.roll(qk, half, axis=1) * sinb)
        for head in range(SWA_HEADS):
            blk = head // HEADS_PER_BLOCK
            piece = qk[:, blk * V7X_LANES:(blk + 1) * V7X_LANES]
            kv_half = head // _SWA_GROUP
            if head % HEADS_PER_BLOCK != kv_half:
                piece = pltpu.roll(piece, HEAD_DIM, axis=1)
            qb_ref[head, rows, :] = jnp.where(
                _half_mask(TOKEN_SUB, kv_half), piece * QK_SCALE_LOG2, 0.0).astype(BF16)
        kb_ref[rows, :] = qk[:, SWA_WIDTH:].astype(BF16)
        vb_ref[rows, 0:V7X_LANES] = r[:, o:o + SWA_KV_WIDTH].astype(BF16)
        vb_ref[rows, V7X_LANES:V_AUG] = ones
        o += SWA_KV_WIDTH
        fu = r[:, o:o + FNET_WIDTH]
        if latent:
            n2_rows = TOKEN_SUB // FFT_N1
            for blk in range(FNET_WIDTH // V7X_LANES):
                fu_scr[part, blk] = fu[:, blk * V7X_LANES:(blk + 1) * V7X_LANES]
            for n1 in range(FFT_N1):
                for blk in range(FNET_WIDTH // V7X_LANES):
                    lo = n1 * FNET_WIDTH + blk * V7X_LANES
                    fu_ref[0, part * n2_rows:(part + 1) * n2_rows, lo:lo + V7X_LANES] = (
                        fu_scr[part, blk, pl.ds(n1, n2_rows, stride=FFT_N1), :].astype(BF16))
        else:
            fu_ref[rows, :] = fu.astype(BF16)


def _rope_tables(seq):
    t = jnp.arange(seq)
    rows = (t // GRID_W).astype(F32)
    cols = (t % GRID_W).astype(F32)
    half = HEAD_DIM // 4
    inv = ROPE_BASE ** (-jnp.arange(half, dtype=F32) / half)
    ang_r = rows[:, None] * inv[None, :]
    ang_c = cols[:, None] * inv[None, :]
    cos_h = jnp.concatenate([jnp.cos(ang_r)] * 2 + [jnp.cos(ang_c)] * 2, axis=1)
    sin_h = jnp.concatenate([jnp.sin(ang_r)] * 2 + [jnp.sin(ang_c)] * 2, axis=1)
    first = (np.arange(HEAD_DIM) % (2 * half)) < half
    sina_h = jnp.where(first[None, :], -sin_h, 0.0)
    sinb_h = jnp.where(first[None, :], 0.0, sin_h)
    two = lambda a: jnp.concatenate([a, a], axis=1)
    return two(cos_h), two(sina_h), two(sinb_h)


def _inproj(x, mod_all, g_pre, w_in_all, layer, mod_row, seq, rope_tables):
    n_tok = x.shape[0]
    latent = rope_tables is not None
    tok_spec, mod_spec, gain_spec = _token_specs(layer, mod_row)
    in_specs = [tok_spec(D_MODEL), mod_spec, gain_spec, _resident((None, D_MODEL, D_IN), (layer, 0, 0))]
    args = [x, mod_all, g_pre, w_in_all]
    tok = lambda w: (tok_spec(w), jax.ShapeDtypeStruct((n_tok, w), BF16))
    outs = [tok(NA_WIDTH), tok(NA_WIDTH), tok(_NA_BLOCKS * V_AUG),
            (pl.BlockSpec((SWA_HEADS, TOKEN_TILE, V7X_LANES), lambda i: (0, i, 0)),
             jax.ShapeDtypeStruct((SWA_HEADS, n_tok, V7X_LANES), BF16)),
            tok(SWA_KV_WIDTH), tok(V_AUG)]
    scratch = []
    if latent:
        tiles_per_seq = seq // TOKEN_TILE
        for tab in rope_tables:
            in_specs.append(pl.BlockSpec((TOKEN_TILE, V7X_LANES), lambda i: (i % tiles_per_seq, 0)))
            args.append(tab)
        rows = TOKEN_TILE // FFT_N1
        outs.append((pl.BlockSpec((1, rows, FFT_N1 * FNET_WIDTH),
                                  lambda i: (i // tiles_per_seq, i % tiles_per_seq, 0)),
                     jax.ShapeDtypeStruct((n_tok // seq, seq // FFT_N1, FFT_N1 * FNET_WIDTH), BF16)))
        scratch.append(pltpu.VMEM((TOKEN_TILE // TOKEN_SUB, FNET_WIDTH // V7X_LANES, TOKEN_SUB, V7X_LANES), F32))
    else:
        outs.append(tok(FNET_WIDTH))
    vmem = D_MODEL * D_IN * 2 + 4 * TOKEN_TILE * D_MODEL * 4 + 4 * TOKEN_TILE * D_IN * 4 + 8 * MIB
    return pl.pallas_call(
        functools.partial(_inproj_kernel, latent=latent),
        grid=(n_tok // TOKEN_TILE,),
        in_specs=in_specs,
        out_specs=[s for s, _ in outs],
        out_shape=[s for _, s in outs],
        scratch_shapes=scratch,
        compiler_params=_params(vmem, 1),
        name="inproj_latent" if latent else "inproj_ctx",
    )(*args)


def _stack_heads(q):
    rows = q.shape[0]
    zero = jnp.zeros_like(q)
    return jnp.concatenate([jnp.where(_half_mask(rows, 0), q, zero),
                            jnp.where(_half_mask(rows, 1), q, zero)], axis=0)


def _attend(q, keys, values, biases, extra_logit=None):
    scores = []
    for k, b in zip(keys, biases):
        s = _nt_dot(q, k)
        scores.append(s if b is None else s + b)
    m = scores[0].max(axis=-1, keepdims=True)
    for s in scores[1:]:
        m = jnp.maximum(m, s.max(axis=-1, keepdims=True))
    if extra_logit is not None:
        m = jnp.maximum(m, extra_logit)
    acc = None
    for s, v in zip(scores, values):
        pv = _dot(jnp.exp2(s - m).astype(BF16), v)
        acc = pv if acc is None else acc + pv
    num = acc[:, :V7X_LANES]
    den = acc[:, V7X_LANES:]
    if extra_logit is not None:
        den = den + jnp.exp2(extra_logit - m)
    return num / den


_NA_Q = NA_SUB_ROWS * GRID_W
_NA_K = NA_KEY_ROWS * GRID_W
_NA_M = HEADS_PER_BLOCK * _NA_Q


def _na_kernel(q_ref, k_ref, v_ref, kc_ref, vc_ref, bias_ref, o_ref, *, grid_rows):
    j = pl.program_id(2)
    n_slabs = grid_rows // NA_SUB_ROWS
    kc = kc_ref[...]
    vc = vc_ref[...]
    for sub in range(NA_SUBS_PER_STEP):
        slab = j * NA_SUBS_PER_STEP + sub
        case = jnp.where(slab == 0, 0, jnp.where(slab == n_slabs - 1, 2, 1))
        start_row = jnp.clip(slab * NA_SUB_ROWS - NA_KR // 2, 0, grid_rows - NA_KEY_ROWS)
        start = pl.multiple_of(start_row * GRID_W, NA_SUB_ROWS * GRID_W)
        q2 = q_ref[sub * _NA_Q:(sub + 1) * _NA_Q, :]
        k = k_ref[pl.ds(start, _NA_K), :]
        v = v_ref[pl.ds(start, _NA_K), :]
        res = []
        for half in range(HEADS_PER_BLOCK):
            q = jnp.where(_half_mask(_NA_Q, half), q2, jnp.zeros_like(q2))
            bias = bias_ref[case, 0, half * _NA_Q:(half + 1) * _NA_Q, :]
            res.append(_attend(q, (k, kc), (v, vc), (bias, None)))
        o_ref[sub * _NA_Q:(sub + 1) * _NA_Q, :] = jnp.where(
            _half_mask(_NA_Q, 0), res[0], res[1]).astype(BF16)


def _na_window_tables(grid_rows):
    n_subs = grid_rows // NA_SUB_ROWS
    kr = min(NA_KR, grid_rows)
    ro = np.zeros((3, NA_SUB_ROWS, NA_KEY_ROWS), np.int32)
    rvalid = np.zeros(ro.shape, bool)
    for ci, u in enumerate((0, min(1, n_subs - 1), n_subs - 1)):
        start_row = int(np.clip(u * NA_SUB_ROWS - NA_KR // 2, 0, grid_rows - NA_KEY_ROWS))
        for l in range(NA_SUB_ROWS):
            rq = u * NA_SUB_ROWS + l
            rs = int(np.clip(rq - kr // 2, 0, grid_rows - kr))
            for i in range(NA_KEY_ROWS):
                rk = start_row + i
                rvalid[ci, l, i] = rs <= rk < rs + kr
                ro[ci, l, i] = int(np.clip(rk - rq + NA_KR - 1, 0, 2 * NA_KR - 2))
    return ro, rvalid


def _na_bias_kernel(rpb_ref, o_ref, toep_ref, *, ro, rvalid, n_dr, n_dc):
    p = pl.program_id(0)
    cq = lax.broadcasted_iota(jnp.int32, (GRID_W, GRID_W), 0)
    ck = lax.broadcasted_iota(jnp.int32, (GRID_W, GRID_W), 1)
    ws = jnp.clip(cq - NA_KW // 2, 0, GRID_W - NA_KW)
    cvalid = (ck >= ws) & (ck < ws + NA_KW)
    co = ck - cq + NA_KW - 1
    neg = jnp.full((GRID_W, GRID_W), NEG_INF, F32)
    n_cases, n_l, n_i = ro.shape
    for hh in range(HEADS_PER_BLOCK):
        head = p * HEADS_PER_BLOCK + hh
        for dr in range(n_dr):
            tile = neg
            for dc in range(n_dc):
                tile = jnp.where(co == dc, rpb_ref[(head * n_dr + dr) * n_dc + dc] * LOG2E, tile)
            toep_ref[dr] = jnp.where(cvalid, tile, neg)
        for c in range(n_cases):
            for l in range(n_l):
                r0 = (hh * n_l + l) * GRID_W
                for i in range(n_i):
                    tile = toep_ref[int(ro[c, l, i])] if rvalid[c, l, i] else neg
                    o_ref[c, 0, r0:r0 + GRID_W, i * GRID_W:(i + 1) * GRID_W] = tile


def _na_bias_tables(rpb, grid_rows):
    ro, rvalid = _na_window_tables(grid_rows)
    n_dr, n_dc = rpb.shape[1], rpb.shape[2]
    return pl.pallas_call(
        functools.partial(_na_bias_kernel, ro=ro, rvalid=rvalid, n_dr=n_dr, n_dc=n_dc),
        grid=(_NA_BLOCKS,),
        in_specs=[pl.BlockSpec(memory_space=pltpu.SMEM)],
        out_specs=pl.BlockSpec((3, 1, _NA_M, _NA_K), lambda p: (0, p, 0, 0)),
        out_shape=jax.ShapeDtypeStruct((3, _NA_BLOCKS, _NA_M, _NA_K), F32),
        scratch_shapes=[pltpu.VMEM((n_dr, GRID_W, GRID_W), F32)],
        compiler_params=_params(4 * 3 * _NA_M * _NA_K * 4, 1),
        name="na_bias",
    )(rpb.astype(F32).reshape(-1))


def _na_attention(qa, ka, va, kac, vac, bias_tab, batch, seq, n_ctx):
    grid_rows = seq // GRID_W
    rows_q = NA_SUBS_PER_STEP * _NA_Q
    n_steps = seq // rows_q
    bias_block = (3, 1, _NA_M, _NA_K)
    vmem = (2 * 3 * _NA_M * _NA_K * 4 + 2 * seq * (V7X_LANES + V_AUG) * 2 + 16 * MIB)
    return pl.pallas_call(
        functools.partial(_na_kernel, grid_rows=grid_rows),
        grid=(batch, _NA_BLOCKS, n_steps),
        in_specs=[
            pl.BlockSpec((rows_q, V7X_LANES), lambda b, p, j: (b * n_steps + j, p)),
            pl.BlockSpec((seq, V7X_LANES), lambda b, p, j: (b, p)),
            pl.BlockSpec((seq, V_AUG), lambda b, p, j: (b, p)),
            pl.BlockSpec((n_ctx, V7X_LANES), lambda b, p, j: (b, p)),
            pl.BlockSpec((n_ctx, V_AUG), lambda b, p, j: (b, p)),
            pl.BlockSpec(bias_block, lambda b, p, j: (0, p, 0, 0)),
        ],
        out_specs=pl.BlockSpec((rows_q, V7X_LANES), lambda b, p, j: (b * n_steps + j, p)),
        out_shape=jax.ShapeDtypeStruct((batch * seq, NA_WIDTH), BF16),
        compiler_params=_params(vmem, 3),
        name="na_attention",
    )(qa, ka, va, kac, vac, bias_tab)


_SWA_SPAN = 3 * SWA_BLOCK


def _gather_swa_heads(res, rows):
    blocks = []
    for blk in range(SWA_HEADS // HEADS_PER_BLOCK):
        halves = []
        for half in range(HEADS_PER_BLOCK):
            head = blk * HEADS_PER_BLOCK + half
            piece = res[head]
            if head // _SWA_GROUP != half:
                piece = pltpu.roll(piece, HEAD_DIM, axis=1)
            halves.append(piece)
        blocks.append(jnp.where(_half_mask(rows, 0), halves[0], halves[1]))
    return jnp.concatenate(blocks, axis=1)


def _swa_kernel(sink_ref, q_ref, k_ref, v_ref, kc_ref, vc_ref, mask_ref, o_ref, *, seq):
    n = pl.program_id(1)
    nb = seq // SWA_BLOCK
    kc = kc_ref[...]
    vc = vc_ref[...]
    for sub in range(SWA_BLOCKS_PER_STEP):
        blk = n * SWA_BLOCKS_PER_STEP + sub
        case = jnp.where(blk == 0, 0, jnp.where(blk == nb - 1, 2, 1))
        start = pl.multiple_of(jnp.clip((blk - 1) * SWA_BLOCK, 0, seq - _SWA_SPAN), SWA_BLOCK)
        k = k_ref[pl.ds(start, _SWA_SPAN), :]
        v = v_ref[pl.ds(start, _SWA_SPAN), :]
        mask = mask_ref[case]
        rows = slice(sub * SWA_BLOCK, (sub + 1) * SWA_BLOCK)
        res = []
        for head in range(SWA_HEADS):
            sink = jnp.full((SWA_BLOCK, 1), sink_ref[head] * LOG2E, F32)
            res.append(_attend(q_ref[head, rows, :], (k, kc), (v, vc), (mask, None), extra_logit=sink))
        o_ref[rows, :] = _gather_swa_heads(res, SWA_BLOCK).astype(BF16)


def _swa_masks(seq):
    nb = seq // SWA_BLOCK
    out = np.zeros((3, SWA_BLOCK, _SWA_SPAN), np.float32)
    for ci, n in enumerate((0, min(1, nb - 1), nb - 1)):
        start = int(np.clip((n - 1) * SWA_BLOCK, 0, seq - _SWA_SPAN))
        kpos = start + np.arange(_SWA_SPAN)[None, :]
        qpos = n * SWA_BLOCK + np.arange(SWA_BLOCK)[:, None]
        out[ci] = np.where(np.abs(kpos - qpos) <= SWA_WINDOW, 0.0, NEG_INF)
    return jnp.asarray(out)


def _swa_attention(qb, kb, vb, kbc, vbc, sink, batch, seq, n_ctx):
    rows = SWA_BLOCKS_PER_STEP * SWA_BLOCK
    ns = seq // rows
    vmem = 4 * seq * (SWA_KV_WIDTH + V_AUG) * 2 + 16 * MIB
    return pl.pallas_call(
        functools.partial(_swa_kernel, seq=seq),
        grid=(batch, ns),
        in_specs=[
            pl.BlockSpec(memory_space=pltpu.SMEM),
            pl.BlockSpec((SWA_HEADS, rows, V7X_LANES), lambda b, n: (0, b * ns + n, 0)),
            pl.BlockSpec((seq, SWA_KV_WIDTH), lambda b, n: (b, 0)),
            pl.BlockSpec((seq, V_AUG), lambda b, n: (b, 0)),
            pl.BlockSpec((n_ctx, SWA_KV_WIDTH), lambda b, n: (b, 0)),
            pl.BlockSpec((n_ctx, V_AUG), lambda b, n: (b, 0)),
            pl.BlockSpec((3, SWA_BLOCK, _SWA_SPAN), lambda b, n: (0, 0, 0)),
        ],
        out_specs=pl.BlockSpec((rows, SWA_WIDTH), lambda b, n: (b * ns + n, 0)),
        out_shape=jax.ShapeDtypeStruct((batch * seq, SWA_WIDTH), BF16),
        compiler_params=_params(vmem, 2),
        name="swa_attention",
    )(sink, qb, kb, vb, kbc, vbc, _swa_masks(seq))


def _dft_constants():
    jc = np.arange(FNET_GROUP_DIM)
    ang_c = 2.0 * np.pi * ((jc[:, None] * jc[None, :]) % FNET_GROUP_DIM) / FNET_GROUP_DIM
    eye = np.eye(FNET_GROUPS)
    bd_cos = np.kron(eye, np.cos(ang_c))
    bd_sin = np.kron(eye, np.sin(ang_c))
    chan = np.concatenate([bd_cos, -bd_sin], axis=1)
    jp = np.arange(FFT_N2)
    ang_p = 2.0 * np.pi * ((jp[:, None] * jp[None, :]) % FFT_N2) / FFT_N2
    cp, sp = np.cos(ang_p), np.sin(ang_p)
    pos = np.block([[cp, sp], [-sp, cp]])
    return jnp.asarray(chan, F32), jnp.asarray(pos, F32)


def _twiddle_tables(n_total):
    n1 = np.arange(FFT_N1)[:, None]
    k2 = np.arange(FFT_N2)[None, :]
    ang = 2.0 * np.pi * ((n1 * k2) % n_total) / n_total
    bc = lambda a: jnp.asarray(np.repeat(a[:, :, None], V7X_LANES, axis=2), F32)
    return bc(np.cos(ang)), bc(np.sin(ang))


def _dft256_of_channels(u, chan, pos):
    z = _dot(u, chan)
    zs = jnp.concatenate([z[:, :FNET_WIDTH], z[:, FNET_WIDTH:]], axis=0).astype(BF16)
    a = _dot(pos, zs)
    return a[:FFT_N2], a[FFT_N2:]


def _weighted_sum(ref, coeffs):
    acc = None
    for n, w in enumerate(coeffs):
        if abs(w) < 1e-9:
            continue
        term = ref[n] if abs(abs(w) - 1.0) < 1e-9 else abs(w) * ref[n]
        if acc is None:
            acc = term if w > 0 else -term
        else:
            acc = acc + term if w > 0 else acc - term
    return acc


def _fnet_kernel(u_ref, chan_ref, pos_ref, twc_ref, tws_ref, o_ref, sre_ref, dim_ref, *, scale):
    chan = chan_ref[...].astype(BF16)
    pos = pos_ref[...].astype(BF16)
    half = FFT_N1 // 2

    def twiddled(n1):
        u = u_ref[0, :, n1 * FNET_WIDTH:(n1 + 1) * FNET_WIDTH]
        a_re, a_im = _dft256_of_channels(u, chan, pos)
        if n1 == 0:
            return a_re, a_im
        c = jnp.concatenate([twc_ref[n1]] * (FNET_WIDTH // V7X_LANES), axis=1)
        s = jnp.concatenate([tws_ref[n1]] * (FNET_WIDTH // V7X_LANES), axis=1)
        return a_re * c + a_im * s, a_im * c - a_re * s

    sre_ref[0] = twiddled(0)[0]
    sre_ref[half] = twiddled(half)[0]
    for n1 in range(1, half):
        lo_re, lo_im = twiddled(n1)
        hi_re, hi_im = twiddled(FFT_N1 - n1)
        sre_ref[n1] = lo_re + hi_re
        dim_ref[n1] = lo_im - hi_im
    for k1 in range(half + 1):
        ang = [2.0 * np.pi * ((n1 * k1) % FFT_N1) / FFT_N1 for n1 in range(half + 1)]
        p = _weighted_sum(sre_ref, [float(np.cos(a)) for a in ang])
        q = _weighted_sum(dim_ref, [0.0] + [float(np.sin(a)) for a in ang[1:half]])
        lo = p if q is None else p + q
        o_ref[k1 * FFT_N2:(k1 + 1) * FFT_N2, :] = (lo * scale).astype(BF16)
        if 0 < k1 < half:
            hi = p - q
            o_ref[(FFT_N1 - k1) * FFT_N2:(FFT_N1 - k1 + 1) * FFT_N2, :] = (hi * scale).astype(BF16)


def _fnet(fu, batch, seq):
    assert seq == FFT_N1 * FFT_N2
    assert fu.shape == (batch, FFT_N2, FFT_N1 * FNET_WIDTH)
    chan, pos = _dft_constants()
    twc, tws = _twiddle_tables(seq)
    scale = float((seq * FNET_GROUP_DIM) ** -0.5)
    vmem = 2 * FFT_N1 * FFT_N2 * FNET_WIDTH * 4 + 8 * seq * FNET_WIDTH * 2 + 16 * MIB
    return pl.pallas_call(
        functools.partial(_fnet_kernel, scale=scale),
        grid=(batch,),
        in_specs=[
            pl.BlockSpec((1, FFT_N2, FFT_N1 * FNET_WIDTH), lambda b: (b, 0, 0)),
            pl.BlockSpec(chan.shape, lambda b: (0, 0)),
            pl.BlockSpec(pos.shape, lambda b: (0, 0)),
            pl.BlockSpec(twc.shape, lambda b: (0, 0, 0)),
            pl.BlockSpec(tws.shape, lambda b: (0, 0, 0)),
        ],
        out_specs=pl.BlockSpec((seq, FNET_WIDTH), lambda b: (b, 0)),
        out_shape=jax.ShapeDtypeStruct((batch * seq, FNET_WIDTH), BF16),
        scratch_shapes=[pltpu.VMEM((FFT_N1 // 2 + 1, FFT_N2, FNET_WIDTH), F32),
                        pltpu.VMEM((FFT_N1 // 2, FFT_N2, FNET_WIDTH), F32)],
        compiler_params=_params(vmem, 1),
        name="fnet",
    )(fu, chan, pos, twc, tws)


def _ctx_mixer_kernel(sink_ref, qa_ref, ka_ref, va_ref, qb_ref, kb_ref, vb_ref, fu_ref, chan_ref,
                      pos_ref, oa_ref, ob_ref, oc_ref, *, n_ctx, scale):
    for blk in range(_NA_BLOCKS):
        sl = slice(blk * V7X_LANES, (blk + 1) * V7X_LANES)
        q = _stack_heads(qa_ref[:, sl])
        res = _attend(q, (ka_ref[:, sl],), (va_ref[:, blk * V_AUG:(blk + 1) * V_AUG],), (None,))
        oa_ref[:, sl] = jnp.where(_half_mask(n_ctx, 0), res[:n_ctx], res[n_ctx:]).astype(BF16)
    k = kb_ref[...]
    v = vb_ref[...]
    res = []
    for head in range(SWA_HEADS):
        sink = jnp.full((n_ctx, 1), sink_ref[head] * LOG2E, F32)
        res.append(_attend(qb_ref[head], (k,), (v,), (None,), extra_logit=sink))
    ob_ref[...] = _gather_swa_heads(res, n_ctx).astype(BF16)
    a_re, _ = _dft256_of_channels(fu_ref[...], chan_ref[...].astype(BF16), pos_ref[...].astype(BF16))
    oc_ref[...] = (a_re * scale).astype(BF16)


def _ctx_mixer(qa, ka, va, qb, kb, vb, fu, sink, batch, n_ctx):
    assert n_ctx == FFT_N2
    chan, pos = _dft_constants()
    scale = float((n_ctx * FNET_GROUP_DIM) ** -0.5)
    tok = lambda w: pl.BlockSpec((n_ctx, w), lambda b: (b, 0))
    widths = (NA_WIDTH, SWA_WIDTH, FNET_WIDTH)
    return pl.pallas_call(
        functools.partial(_ctx_mixer_kernel, n_ctx=n_ctx, scale=scale),
        grid=(batch,),
        in_specs=[
            pl.BlockSpec(memory_space=pltpu.SMEM),
            tok(NA_WIDTH), tok(NA_WIDTH), tok(_NA_BLOCKS * V_AUG),
            pl.BlockSpec((SWA_HEADS, n_ctx, V7X_LANES), lambda b: (0, b, 0)),
            tok(SWA_KV_WIDTH), tok(V_AUG), tok(FNET_WIDTH),
            pl.BlockSpec(chan.shape, lambda b: (0, 0)),
            pl.BlockSpec(pos.shape, lambda b: (0, 0)),
        ],
        out_specs=[tok(w) for w in widths],
        out_shape=[jax.ShapeDtypeStruct((batch * n_ctx, w), BF16) for w in widths],
        compiler_params=_params(24 * MIB, 1),
        name="ctx_mixer",
    )(sink, qa, ka, va, qb, kb, vb, fu, chan, pos)


def kernel(x, c, ctx, c_ctx, w_mod, b_mod, g_pre, g_post, w_ffn_in, w_ffn_out, w_in, w_out, na_rpb, swa_sink):
    batch, seq, _ = x.shape
    n_ctx = ctx.shape[1]
    depth = w_mod.shape[0]
    assert seq % (NA_SUBS_PER_STEP * NA_SUB_ROWS * GRID_W) == 0 and seq % TOKEN_TILE == 0
    assert seq % (SWA_BLOCKS_PER_STEP * SWA_BLOCK) == 0 and (batch * n_ctx) % TOKEN_TILE == 0

    mod_rows = -(-(batch + 1) // 8) * 8
    c_rows = jnp.concatenate(
        [c, c_ctx[None, :], jnp.zeros((mod_rows - batch - 1, D_MODEL), F32)], axis=0)
    mod_all = _modulation(c_rows, w_mod, b_mod).reshape(depth, mod_rows, N_MOD, D_MODEL)
    tiles_per_seq = seq // TOKEN_TILE
    latent_row = lambda i: i // tiles_per_seq
    ctx_row = lambda i: batch

    w1_all, w2_all = w_ffn_in.astype(BF16), w_ffn_out.astype(BF16)
    w_in_all, w_out_all = w_in.astype(BF16), w_out.astype(BF16)
    rope_tables = _rope_tables(seq)
    xl = x.reshape(batch * seq, D_MODEL)
    xc = ctx.reshape(batch * n_ctx, D_MODEL)

    for layer in range(depth):
        sink = swa_sink[layer].astype(F32)
        last = layer == depth - 1

        xl = _ffn(xl, mod_all, g_pre, g_post, w1_all, w2_all, layer, 0, latent_row)
        xc = _ffn(xc, mod_all, g_pre, g_post, w1_all, w2_all, layer, 0, ctx_row)

        qa, ka, va, qb, kb, vb, fu = _inproj(xl, mod_all, g_pre, w_in_all, layer, latent_row, seq, rope_tables)
        qac, kac, vac, qbc, kbc, vbc, fuc = _inproj(xc, mod_all, g_pre, w_in_all, layer, ctx_row, seq, None)

        bias_tab = _na_bias_tables(na_rpb[layer], seq // GRID_W)
        oa = _na_attention(qa, ka, va, kac, vac, bias_tab, batch, seq, n_ctx)
        ob = _swa_attention(qb, kb, vb, kbc, vbc, sink, batch, seq, n_ctx)
        oc = _fnet(fu, batch, seq)
        xl = _ffn(xl, mod_all, g_pre, g_post, w1_all, w2_all, layer, 1, latent_row,
                  mix=(oa, ob, oc, w_out_all))

        if not last:
            oac, obc, occ = _ctx_mixer(qac, kac, vac, qbc, kbc, vbc, fuc, sink, batch, n_ctx)
            xc = _ffn(xc, mod_all, g_pre, g_post, w1_all, w2_all, layer, 1, ctx_row,
                      mix=(oac, obc, occ, w_out_all))

    return xl.reshape(batch, seq, D_MODEL)
```

```python
import functools
import math

import numpy as np
import jax
import jax.numpy as jnp
from jax import lax
from jax.experimental import pallas as pl
from jax.experimental.pallas import tpu as pltpu

F32 = jnp.float32
BF16 = jnp.bfloat16

D_MODEL = 1024
GRID_W = 64
HEAD_DIM = 64
NA_HEADS = 6
NA_KR = 8
NA_KW = 16
SWA_HEADS = 6
SWA_KV_HEADS = 2
SWA_WINDOW = 128
SWA_BLOCK = 128
FNET_GROUPS = 4
FNET_GROUP_DIM = 64
NA_WIDTH = NA_HEADS * HEAD_DIM
SWA_WIDTH = SWA_HEADS * HEAD_DIM
SWA_KV_WIDTH = SWA_KV_HEADS * HEAD_DIM
FNET_WIDTH = FNET_GROUPS * FNET_GROUP_DIM
D_IN = 3 * NA_WIDTH + SWA_WIDTH + 2 * SWA_KV_WIDTH + FNET_WIDTH
D_FF = 256 * ((8 * D_MODEL // 3 + 255) // 256)
N_SUB = 3
N_MOD = 3 * N_SUB
MACARON_WEIGHT = 0.5
ROPE_BASE = 10000.0
RMS_EPS = 1e-6
NEG_INF = -1e30
LOG2E = math.log2(math.e)
QK_SCALE_LOG2 = HEAD_DIM ** -0.5 * LOG2E

V7X_LANES = 128
V7X_VMEM_LIMIT_BYTES = 60000 * 1024
MIB = 1024 * 1024

TOKEN_TILE = 1024
TOKEN_SUB = 512
FFN_CHUNK = 256
MOD_TILE = 1152
NA_SUB_ROWS = 4
NA_SUBS_PER_STEP = 16
SWA_BLOCKS_PER_STEP = 8
NA_KEY_ROWS = NA_SUB_ROWS + NA_KR
FFT_N1 = 16
FFT_N2 = 256

HEADS_PER_BLOCK = V7X_LANES // HEAD_DIM
V_AUG = 2 * V7X_LANES


def _params(vmem_bytes, n_grid):
    limit = int(min(max(vmem_bytes, 16 * MIB), V7X_VMEM_LIMIT_BYTES))
    return pltpu.CompilerParams(dimension_semantics=("arbitrary",) * n_grid, vmem_limit_bytes=limit)


def _rms(x, g):
    return x * lax.rsqrt(jnp.mean(x * x, axis=-1, keepdims=True) + RMS_EPS) * g


def _nt_dot(a, b):
    return lax.dot_general(a, b, (((1,), (1,)), ((), ())), preferred_element_type=F32)


def _dot(a, b):
    return jnp.dot(a, b, preferred_element_type=F32)


def _mod_kernel(c_ref, w_ref, b_ref, o_ref):
    a = c_ref[...]
    act = a / (1.0 + jnp.exp(-a))
    o_ref[0] = _dot(act.astype(BF16), w_ref[0].astype(BF16)) + b_ref[0]


def _modulation(c_rows, w_mod, b_mod):
    depth, _, n_out = w_mod.shape
    rows = c_rows.shape[0]
    return pl.pallas_call(
        _mod_kernel,
        grid=(depth, n_out // MOD_TILE),
        in_specs=[
            pl.BlockSpec((rows, D_MODEL), lambda l, j: (0, 0)),
            pl.BlockSpec((1, D_MODEL, MOD_TILE), lambda l, j: (l, 0, j)),
            pl.BlockSpec((1, 1, MOD_TILE), lambda l, j: (l, 0, j)),
        ],
        out_specs=pl.BlockSpec((1, rows, MOD_TILE), lambda l, j: (l, 0, j)),
        out_shape=jax.ShapeDtypeStruct((depth, rows, n_out), F32),
        compiler_params=_params(4 * D_MODEL * MOD_TILE * 4, 2),
        name="modulation",
    )(c_rows, w_mod, b_mod.reshape(depth, 1, n_out))


def _token_specs(layer, mod_row):
    tok = lambda w: pl.BlockSpec((TOKEN_TILE, w), lambda i: (i, 0))
    mod = pl.BlockSpec((None, 1, N_MOD, D_MODEL), lambda i: (layer, mod_row(i), 0, 0))
    gain = pl.BlockSpec((None, N_SUB, D_MODEL), lambda i: (layer, 0, 0))
    return tok, mod, gain


def _resident(shape, index):
    return pl.BlockSpec(shape, lambda i: index, pipeline_mode=pl.Buffered(1))


def _ffn_kernel(*refs, sub, mix):
    if mix:
        (x_ref, mod_ref, gpre_ref, gpost_ref, oa_ref, ob_ref, oc_ref, wo_ref, w1_ref, w2_ref,
         o_ref, act_ref) = refs
    else:
        x_ref, mod_ref, gpre_ref, gpost_ref, w1_ref, w2_ref, o_ref, act_ref = refs
    shift = mod_ref[0, 3 * sub:3 * sub + 1, :]
    scale = mod_ref[0, 3 * sub + 1:3 * sub + 2, :]
    gate = mod_ref[0, 3 * sub + 2:3 * sub + 3, :]
    for part in range(TOKEN_TILE // TOKEN_SUB):
        rows = slice(part * TOKEN_SUB, (part + 1) * TOKEN_SUB)
        x = x_ref[rows, :]
        if mix:
            mixed = jnp.concatenate([oa_ref[rows, :], ob_ref[rows, :], oc_ref[rows, :]], axis=1)
            x = x + mod_ref[0, 5:6, :] * _rms(_dot(mixed, wo_ref[...]), gpost_ref[1:2, :])
        h = _rms(x, gpre_ref[sub:sub + 1, :]) * (1.0 + scale) + shift
        hb = h.astype(BF16)
        for c in range(D_FF // FFN_CHUNK):
            lo = c * FFN_CHUNK
            g = _dot(hb, w1_ref[:, lo:lo + FFN_CHUNK])
            u = _dot(hb, w1_ref[:, D_FF + lo:D_FF + lo + FFN_CHUNK])
            act_ref[part, :, lo:lo + FFN_CHUNK] = (g / (1.0 + jnp.exp(-g)) * u).astype(BF16)
        y = _dot(act_ref[part], w2_ref[...])
        o_ref[rows, :] = x + MACARON_WEIGHT * gate * _rms(y, gpost_ref[sub:sub + 1, :])


def _ffn(x, mod_all, g_pre, g_post, w1_all, w2_all, layer, which, mod_row, mix=None):
    n_tok = x.shape[0]
    sub = 2 * which
    tok, mod_spec, gain_spec = _token_specs(layer, mod_row)
    in_specs = [tok(D_MODEL), mod_spec, gain_spec, gain_spec]
    args = [x, mod_all, g_pre, g_post]
    vmem = ((D_MODEL * 2 * D_FF + D_FF * D_MODEL) * 2 + 4 * TOKEN_TILE * D_MODEL * 4
            + TOKEN_TILE * D_FF * 2 + 12 * MIB)
    if mix is not None:
        oa, ob, oc, w_out_all = mix
        in_specs += [tok(NA_WIDTH), tok(SWA_WIDTH), tok(FNET_WIDTH),
                     _resident((None, D_MODEL, D_MODEL), (layer, 0, 0))]
        args += [oa, ob, oc, w_out_all]
        vmem += D_MODEL * D_MODEL * 2 + 2 * TOKEN_TILE * D_MODEL * 2
    in_specs += [_resident((None, None, D_MODEL, 2 * D_FF), (layer, which, 0, 0)),
                 _resident((None, None, D_FF, D_MODEL), (layer, which, 0, 0))]
    args += [w1_all, w2_all]
    return pl.pallas_call(
        functools.partial(_ffn_kernel, sub=sub, mix=mix is not None),
        grid=(n_tok // TOKEN_TILE,),
        in_specs=in_specs,
        out_specs=tok(D_MODEL),
        out_shape=jax.ShapeDtypeStruct((n_tok, D_MODEL), F32),
        scratch_shapes=[pltpu.VMEM((TOKEN_TILE // TOKEN_SUB, TOKEN_SUB, D_FF), BF16)],
        compiler_params=_params(vmem, 1),
        name="mix_ffn" if mix is not None else "ffn",
    )(*args)


_NA_BLOCKS = NA_HEADS // HEADS_PER_BLOCK
_SWA_GROUP = SWA_HEADS // SWA_KV_HEADS


def _half_mask(rows, half):
    lane = lax.broadcasted_iota(jnp.int32, (rows, V7X_LANES), 1)
    return (lane >= HEAD_DIM) if half else (lane < HEAD_DIM)


def _inproj_kernel(*refs, latent):
    if latent:
        (x_ref, mod_ref, gpre_ref, w_ref, cos_ref, sina_ref, sinb_ref,
         qa_ref, ka_ref, va_ref, qb_ref, kb_ref, vb_ref, fu_ref, fu_scr) = refs
    else:
        (x_ref, mod_ref, gpre_ref, w_ref,
         qa_ref, ka_ref, va_ref, qb_ref, kb_ref, vb_ref, fu_ref) = refs
    shift = mod_ref[0, 3:4, :]
    scale = mod_ref[0, 4:5, :]
    ones = jnp.ones((TOKEN_SUB, V7X_LANES), BF16)
    for part in range(TOKEN_TILE // TOKEN_SUB):
        rows = slice(part * TOKEN_SUB, (part + 1) * TOKEN_SUB)
        h = _rms(x_ref[rows, :], gpre_ref[1:2, :]) * (1.0 + scale) + shift
        r = _dot(h.astype(BF16), w_ref[...])
        o = 0
        qa_ref[rows, :] = (r[:, o:o + NA_WIDTH] * QK_SCALE_LOG2).astype(BF16)
        o += NA_WIDTH
        ka_ref[rows, :] = r[:, o:o + NA_WIDTH].astype(BF16)
        o += NA_WIDTH
        for blk in range(_NA_BLOCKS):
            lo = o + blk * V7X_LANES
            va_ref[rows, blk * V_AUG:blk * V_AUG + V7X_LANES] = r[:, lo:lo + V7X_LANES].astype(BF16)
            va_ref[rows, blk * V_AUG + V7X_LANES:(blk + 1) * V_AUG] = ones
        o += NA_WIDTH
        qk = r[:, o:o + SWA_WIDTH + SWA_KV_WIDTH]
        o += SWA_WIDTH + SWA_KV_WIDTH
        if latent:
            reps = (SWA_WIDTH + SWA_KV_WIDTH) // V7X_LANES
            width = qk.shape[1]
            cos = jnp.concatenate([cos_ref[rows, :]] * reps, axis=1)
            sina = jnp.concatenate([sina_ref[rows, :]] * reps, axis=1)
            sinb = jnp.concatenate([sinb_ref[rows, :]] * reps, axis=1)
            half = HEAD_DIM // 4
            qk = (qk * cos + pltpu.roll(qk, width - half, axis=1) * sina
                  + pltpu.roll(qk, half, axis=1) * sinb)
        for head in range(SWA_HEADS):
            blk = head // HEADS_PER_BLOCK
            piece = qk[:, blk * V7X_LANES:(blk + 1) * V7X_LANES]
            kv_half = head // _SWA_GROUP
            if head % HEADS_PER_BLOCK != kv_half:
                piece = pltpu.roll(piece, HEAD_DIM, axis=1)
            qb_ref[head, rows, :] = jnp.where(
                _half_mask(TOKEN_SUB, kv_half), piece * QK_SCALE_LOG2, 0.0).astype(BF16)
        kb_ref[rows, :] = qk[:, SWA_WIDTH:].astype(BF16)
        vb_ref[rows, 0:V7X_LANES] = r[:, o:o + SWA_KV_WIDTH].astype(BF16)
        vb_ref[rows, V7X_LANES:V_AUG] = ones
        o += SWA_KV_WIDTH
        fu = r[:, o:o + FNET_WIDTH]
        if latent:
            n2_rows = TOKEN_SUB // FFT_N1
            for blk in range(FNET_WIDTH // V7X_LANES):
                fu_scr[part, blk] = fu[:, blk * V7X_LANES:(blk + 1) * V7X_LANES]
            for n1 in range(FFT_N1):
                for blk in range(FNET_WIDTH // V7X_LANES):
                    lo = n1 * FNET_WIDTH + blk * V7X_LANES
                    fu_ref[0, part * n2_rows:(part + 1) * n2_rows, lo:lo + V7X_LANES] = (
                        fu_scr[part, blk, pl.ds(n1, n2_rows, stride=FFT_N1), :].astype(BF16))
        else:
            fu_ref[rows, :] = fu.astype(BF16)


def _rope_tables(seq):
    n_rows = seq // GRID_W
    half = HEAD_DIM // 4
    inv = ROPE_BASE ** (-jnp.arange(half, dtype=F32) / half)
    ang_r = jnp.arange(n_rows, dtype=F32)[:, None] * inv[None, :]
    ang_c = jnp.arange(GRID_W, dtype=F32)[:, None] * inv[None, :]
    by_row = lambda a: jnp.repeat(a, GRID_W, axis=0)
    by_col = lambda a: jnp.tile(a, (n_rows, 1))
    cos_h = jnp.concatenate([by_row(jnp.cos(ang_r))] * 2 + [by_col(jnp.cos(ang_c))] * 2, axis=1)
    sin_h = jnp.concatenate([by_row(jnp.sin(ang_r))] * 2 + [by_col(jnp.sin(ang_c))] * 2, axis=1)
    first = (np.arange(HEAD_DIM) % (2 * half)) < half
    sina_h = jnp.where(first[None, :], -sin_h, 0.0)
    sinb_h = jnp.where(first[None, :], 0.0, sin_h)
    two = lambda a: jnp.concatenate([a, a], axis=1)
    return two(cos_h), two(sina_h), two(sinb_h)


def _inproj(x, mod_all, g_pre, w_in_all, layer, mod_row, seq, rope_tables):
    n_tok = x.shape[0]
    latent = rope_tables is not None
    tok_spec, mod_spec, gain_spec = _token_specs(layer, mod_row)
    in_specs = [tok_spec(D_MODEL), mod_spec, gain_spec, _resident((None, D_MODEL, D_IN), (layer, 0, 0))]
    args = [x, mod_all, g_pre, w_in_all]
    tok = lambda w: (tok_spec(w), jax.ShapeDtypeStruct((n_tok, w), BF16))
    outs = [tok(NA_WIDTH), tok(NA_WIDTH), tok(_NA_BLOCKS * V_AUG),
            (pl.BlockSpec((SWA_HEADS, TOKEN_TILE, V7X_LANES), lambda i: (0, i, 0)),
             jax.ShapeDtypeStruct((SWA_HEADS, n_tok, V7X_LANES), BF16)),
            tok(SWA_KV_WIDTH), tok(V_AUG)]
    scratch = []
    if latent:
        tiles_per_seq = seq // TOKEN_TILE
        for tab in rope_tables:
            in_specs.append(pl.BlockSpec((TOKEN_TILE, V7X_LANES), lambda i: (i % tiles_per_seq, 0)))
            args.append(tab)
        rows = TOKEN_TILE // FFT_N1
        outs.append((pl.BlockSpec((1, rows, FFT_N1 * FNET_WIDTH),
                                  lambda i: (i // tiles_per_seq, i % tiles_per_seq, 0)),
                     jax.ShapeDtypeStruct((n_tok // seq, seq // FFT_N1, FFT_N1 * FNET_WIDTH), BF16)))
        scratch.append(pltpu.VMEM((TOKEN_TILE // TOKEN_SUB, FNET_WIDTH // V7X_LANES, TOKEN_SUB, V7X_LANES), F32))
    else:
        outs.append(tok(FNET_WIDTH))
    vmem = D_MODEL * D_IN * 2 + 4 * TOKEN_TILE * D_MODEL * 4 + 4 * TOKEN_TILE * D_IN * 4 + 8 * MIB
    return pl.pallas_call(
        functools.partial(_inproj_kernel, latent=latent),
        grid=(n_tok // TOKEN_TILE,),
        in_specs=in_specs,
        out_specs=[s for s, _ in outs],
        out_shape=[s for _, s in outs],
        scratch_shapes=scratch,
        compiler_params=_params(vmem, 1),
        name="inproj_latent" if latent else "inproj_ctx",
    )(*args)


def _stack_heads(q):
    rows = q.shape[0]
    zero = jnp.zeros_like(q)
    return jnp.concatenate([jnp.where(_half_mask(rows, 0), q, zero),
                            jnp.where(_half_mask(rows, 1), q, zero)], axis=0)


def _attend(q, keys, values, biases, extra_logit=None):
    scores = []
    for k, b in zip(keys, biases):
        s = _nt_dot(q, k)
        scores.append(s if b is None else s + b)
    m = scores[0].max(axis=-1, keepdims=True)
    for s in scores[1:]:
        m = jnp.maximum(m, s.max(axis=-1, keepdims=True))
    if extra_logit is not None:
        m = jnp.maximum(m, extra_logit)
    acc = None
    for s, v in zip(scores, values):
        pv = _dot(jnp.exp2(s - m).astype(BF16), v)
        acc = pv if acc is None else acc + pv
    num = acc[:, :V7X_LANES]
    den = acc[:, V7X_LANES:]
    if extra_logit is not None:
        den = den + jnp.exp2(extra_logit - m)
    return num / den


_NA_Q = NA_SUB_ROWS * GRID_W
_NA_K = NA_KEY_ROWS * GRID_W
_NA_M = HEADS_PER_BLOCK * _NA_Q


def _na_kernel(q_ref, k_ref, v_ref, kc_ref, vc_ref, bias_ref, o_ref, *, grid_rows):
    j = pl.program_id(2)
    n_slabs = grid_rows // NA_SUB_ROWS
    kc = kc_ref[...]
    vc = vc_ref[...]
    for sub in range(NA_SUBS_PER_STEP):
        slab = j * NA_SUBS_PER_STEP + sub
        case = jnp.where(slab == 0, 0, jnp.where(slab == n_slabs - 1, 2, 1))
        start_row = jnp.clip(slab * NA_SUB_ROWS - NA_KR // 2, 0, grid_rows - NA_KEY_ROWS)
        start = pl.multiple_of(start_row * GRID_W, NA_SUB_ROWS * GRID_W)
        q2 = q_ref[sub * _NA_Q:(sub + 1) * _NA_Q, :]
        k = k_ref[pl.ds(start, _NA_K), :]
        v = v_ref[pl.ds(start, _NA_K), :]
        res = []
        for half in range(HEADS_PER_BLOCK):
            q = jnp.where(_half_mask(_NA_Q, half), q2, jnp.zeros_like(q2))
            bias = bias_ref[case, 0, half * _NA_Q:(half + 1) * _NA_Q, :]
            res.append(_attend(q, (k, kc), (v, vc), (bias, None)))
        o_ref[sub * _NA_Q:(sub + 1) * _NA_Q, :] = jnp.where(
            _half_mask(_NA_Q, 0), res[0], res[1]).astype(BF16)


def _na_window_tables(grid_rows):
    n_subs = grid_rows // NA_SUB_ROWS
    kr = min(NA_KR, grid_rows)
    ro = np.zeros((3, NA_SUB_ROWS, NA_KEY_ROWS), np.int32)
    rvalid = np.zeros(ro.shape, bool)
    for ci, u in enumerate((0, min(1, n_subs - 1), n_subs - 1)):
        start_row = int(np.clip(u * NA_SUB_ROWS - NA_KR // 2, 0, grid_rows - NA_KEY_ROWS))
        for l in range(NA_SUB_ROWS):
            rq = u * NA_SUB_ROWS + l
            rs = int(np.clip(rq - kr // 2, 0, grid_rows - kr))
            for i in range(NA_KEY_ROWS):
                rk = start_row + i
                rvalid[ci, l, i] = rs <= rk < rs + kr
                ro[ci, l, i] = int(np.clip(rk - rq + NA_KR - 1, 0, 2 * NA_KR - 2))
    return ro, rvalid


def _na_bias_kernel(rpb_ref, o_ref, toep_ref, *, ro, rvalid, n_dr, n_dc):
    p = pl.program_id(0)
    cq = lax.broadcasted_iota(jnp.int32, (GRID_W, GRID_W), 0)
    ck = lax.broadcasted_iota(jnp.int32, (GRID_W, GRID_W), 1)
    ws = jnp.clip(cq - NA_KW // 2, 0, GRID_W - NA_KW)
    cvalid = (ck >= ws) & (ck < ws + NA_KW)
    co = ck - cq + NA_KW - 1
    neg = jnp.full((GRID_W, GRID_W), NEG_INF, F32)
    n_cases, n_l, n_i = ro.shape
    for hh in range(HEADS_PER_BLOCK):
        head = p * HEADS_PER_BLOCK + hh
        for dr in range(n_dr):
            tile = neg
            for dc in range(n_dc):
                tile = jnp.where(co == dc, rpb_ref[(head * n_dr + dr) * n_dc + dc] * LOG2E, tile)
            toep_ref[dr] = jnp.where(cvalid, tile, neg)
        for c in range(n_cases):
            for l in range(n_l):
                r0 = (hh * n_l + l) * GRID_W
                for i in range(n_i):
                    tile = toep_ref[int(ro[c, l, i])] if rvalid[c, l, i] else neg
                    o_ref[c, 0, r0:r0 + GRID_W, i * GRID_W:(i + 1) * GRID_W] = tile


def _na_bias_tables(rpb, grid_rows):
    ro, rvalid = _na_window_tables(grid_rows)
    n_dr, n_dc = rpb.shape[1], rpb.shape[2]
    return pl.pallas_call(
        functools.partial(_na_bias_kernel, ro=ro, rvalid=rvalid, n_dr=n_dr, n_dc=n_dc),
        grid=(_NA_BLOCKS,),
        in_specs=[pl.BlockSpec(memory_space=pltpu.SMEM)],
        out_specs=pl.BlockSpec((3, 1, _NA_M, _NA_K), lambda p: (0, p, 0, 0)),
        out_shape=jax.ShapeDtypeStruct((3, _NA_BLOCKS, _NA_M, _NA_K), F32),
        scratch_shapes=[pltpu.VMEM((n_dr, GRID_W, GRID_W), F32)],
        compiler_params=_params(4 * 3 * _NA_M * _NA_K * 4, 1),
        name="na_bias",
    )(rpb.astype(F32).reshape(-1))


def _na_attention(qa, ka, va, kac, vac, bias_tab, batch, seq, n_ctx):
    grid_rows = seq // GRID_W
    rows_q = NA_SUBS_PER_STEP * _NA_Q
    n_steps = seq // rows_q
    bias_block = (3, 1, _NA_M, _NA_K)
    vmem = (2 * 3 * _NA_M * _NA_K * 4 + 2 * seq * (V7X_LANES + V_AUG) * 2 + 16 * MIB)
    return pl.pallas_call(
        functools.partial(_na_kernel, grid_rows=grid_rows),
        grid=(batch, _NA_BLOCKS, n_steps),
        in_specs=[
            pl.BlockSpec((rows_q, V7X_LANES), lambda b, p, j: (b * n_steps + j, p)),
            pl.BlockSpec((seq, V7X_LANES), lambda b, p, j: (b, p)),
            pl.BlockSpec((seq, V_AUG), lambda b, p, j: (b, p)),
            pl.BlockSpec((n_ctx, V7X_LANES), lambda b, p, j: (b, p)),
            pl.BlockSpec((n_ctx, V_AUG), lambda b, p, j: (b, p)),
            pl.BlockSpec(bias_block, lambda b, p, j: (0, p, 0, 0)),
        ],
        out_specs=pl.BlockSpec((rows_q, V7X_LANES), lambda b, p, j: (b * n_steps + j, p)),
        out_shape=jax.ShapeDtypeStruct((batch * seq, NA_WIDTH), BF16),
        compiler_params=_params(vmem, 3),
        name="na_attention",
    )(qa, ka, va, kac, vac, bias_tab)


_SWA_SPAN = 3 * SWA_BLOCK


def _gather_swa_heads(res, rows):
    blocks = []
    for blk in range(SWA_HEADS // HEADS_PER_BLOCK):
        halves = []
        for half in range(HEADS_PER_BLOCK):
            head = blk * HEADS_PER_BLOCK + half
            piece = res[head]
            if head // _SWA_GROUP != half:
                piece = pltpu.roll(piece, HEAD_DIM, axis=1)
            halves.append(piece)
        blocks.append(jnp.where(_half_mask(rows, 0), halves[0], halves[1]))
    return jnp.concatenate(blocks, axis=1)


def _swa_kernel(sink_ref, q_ref, k_ref, v_ref, kc_ref, vc_ref, mask_ref, o_ref, *, seq):
    n = pl.program_id(1)
    nb = seq // SWA_BLOCK
    kc = kc_ref[...]
    vc = vc_ref[...]
    for sub in range(SWA_BLOCKS_PER_STEP):
        blk = n * SWA_BLOCKS_PER_STEP + sub
        case = jnp.where(blk == 0, 0, jnp.where(blk == nb - 1, 2, 1))
        start = pl.multiple_of(jnp.clip((blk - 1) * SWA_BLOCK, 0, seq - _SWA_SPAN), SWA_BLOCK)
        k = k_ref[pl.ds(start, _SWA_SPAN), :]
        v = v_ref[pl.ds(start, _SWA_SPAN), :]
        mask = mask_ref[case]
        rows = slice(sub * SWA_BLOCK, (sub + 1) * SWA_BLOCK)
        res = []
        for head in range(SWA_HEADS):
            sink = jnp.full((SWA_BLOCK, 1), sink_ref[head] * LOG2E, F32)
            res.append(_attend(q_ref[head, rows, :], (k, kc), (v, vc), (mask, None), extra_logit=sink))
        o_ref[rows, :] = _gather_swa_heads(res, SWA_BLOCK).astype(BF16)


def _swa_masks(seq):
    nb = seq // SWA_BLOCK
    out = np.zeros((3, SWA_BLOCK, _SWA_SPAN), np.float32)
    for ci, n in enumerate((0, min(1, nb - 1), nb - 1)):
        start = int(np.clip((n - 1) * SWA_BLOCK, 0, seq - _SWA_SPAN))
        kpos = start + np.arange(_SWA_SPAN)[None, :]
        qpos = n * SWA_BLOCK + np.arange(SWA_BLOCK)[:, None]
        out[ci] = np.where(np.abs(kpos - qpos) <= SWA_WINDOW, 0.0, NEG_INF)
    return jnp.asarray(out)


def _swa_attention(qb, kb, vb, kbc, vbc, sink, batch, seq, n_ctx):
    rows = SWA_BLOCKS_PER_STEP * SWA_BLOCK
    ns = seq // rows
    vmem = 4 * seq * (SWA_KV_WIDTH + V_AUG) * 2 + 16 * MIB
    return pl.pallas_call(
        functools.partial(_swa_kernel, seq=seq),
        grid=(batch, ns),
        in_specs=[
            pl.BlockSpec(memory_space=pltpu.SMEM),
            pl.BlockSpec((SWA_HEADS, rows, V7X_LANES), lambda b, n: (0, b * ns + n, 0)),
            pl.BlockSpec((seq, SWA_KV_WIDTH), lambda b, n: (b, 0)),
            pl.BlockSpec((seq, V_AUG), lambda b, n: (b, 0)),
            pl.BlockSpec((n_ctx, SWA_KV_WIDTH), lambda b, n: (b, 0)),
            pl.BlockSpec((n_ctx, V_AUG), lambda b, n: (b, 0)),
            pl.BlockSpec((3, SWA_BLOCK, _SWA_SPAN), lambda b, n: (0, 0, 0)),
        ],
        out_specs=pl.BlockSpec((rows, SWA_WIDTH), lambda b, n: (b * ns + n, 0)),
        out_shape=jax.ShapeDtypeStruct((batch * seq, SWA_WIDTH), BF16),
        compiler_params=_params(vmem, 2),
        name="swa_attention",
    )(sink, qb, kb, vb, kbc, vbc, _swa_masks(seq))


def _dft_constants():
    jc = np.arange(FNET_GROUP_DIM)
    ang_c = 2.0 * np.pi * ((jc[:, None] * jc[None, :]) % FNET_GROUP_DIM) / FNET_GROUP_DIM
    eye = np.eye(FNET_GROUPS)
    bd_cos = np.kron(eye, np.cos(ang_c))
    bd_sin = np.kron(eye, np.sin(ang_c))
    chan = np.concatenate([bd_cos, -bd_sin], axis=1)
    jp = np.arange(FFT_N2)
    ang_p = 2.0 * np.pi * ((jp[:, None] * jp[None, :]) % FFT_N2) / FFT_N2
    cp, sp = np.cos(ang_p), np.sin(ang_p)
    pos = np.block([[cp, sp], [-sp, cp]])
    return jnp.asarray(chan, F32), jnp.asarray(pos, F32)


def _twiddle_tables(n_total):
    n1 = np.arange(FFT_N1)[:, None]
    k2 = np.arange(FFT_N2)[None, :]
    ang = 2.0 * np.pi * ((n1 * k2) % n_total) / n_total
    bc = lambda a: jnp.asarray(np.repeat(a[:, :, None], V7X_LANES, axis=2), F32)
    return bc(np.cos(ang)), bc(np.sin(ang))


def _dft256_of_channels(u, chan, pos):
    z = _dot(u, chan)
    zs = jnp.concatenate([z[:, :FNET_WIDTH], z[:, FNET_WIDTH:]], axis=0).astype(BF16)
    a = _dot(pos, zs)
    return a[:FFT_N2], a[FFT_N2:]


def _weighted_sum(ref, coeffs):
    acc = None
    for n, w in enumerate(coeffs):
        if abs(w) < 1e-9:
            continue
        term = ref[n] if abs(abs(w) - 1.0) < 1e-9 else abs(w) * ref[n]
        if acc is None:
            acc = term if w > 0 else -term
        else:
            acc = acc + term if w > 0 else acc - term
    return acc


def _fnet_kernel(u_ref, chan_ref, pos_ref, twc_ref, tws_ref, o_ref, sre_ref, dim_ref, *, scale):
    chan = chan_ref[...].astype(BF16)
    pos = pos_ref[...].astype(BF16)
    half = FFT_N1 // 2

    def twiddled(n1):
        u = u_ref[0, :, n1 * FNET_WIDTH:(n1 + 1) * FNET_WIDTH]
        a_re, a_im = _dft256_of_channels(u, chan, pos)
        if n1 == 0:
            return a_re, a_im
        c = jnp.concatenate([twc_ref[n1]] * (FNET_WIDTH // V7X_LANES), axis=1)
        s = jnp.concatenate([tws_ref[n1]] * (FNET_WIDTH // V7X_LANES), axis=1)
        return a_re * c + a_im * s, a_im * c - a_re * s

    sre_ref[0] = twiddled(0)[0]
    sre_ref[half] = twiddled(half)[0]
    for n1 in range(1, half):
        lo_re, lo_im = twiddled(n1)
        hi_re, hi_im = twiddled(FFT_N1 - n1)
        sre_ref[n1] = lo_re + hi_re
        dim_ref[n1] = lo_im - hi_im
    for k1 in range(half + 1):
        ang = [2.0 * np.pi * ((n1 * k1) % FFT_N1) / FFT_N1 for n1 in range(half + 1)]
        p = _weighted_sum(sre_ref, [float(np.cos(a)) for a in ang])
        q = _weighted_sum(dim_ref, [0.0] + [float(np.sin(a)) for a in ang[1:half]])
        lo = p if q is None else p + q
        o_ref[k1 * FFT_N2:(k1 + 1) * FFT_N2, :] = (lo * scale).astype(BF16)
        if 0 < k1 < half:
            hi = p - q
            o_ref[(FFT_N1 - k1) * FFT_N2:(FFT_N1 - k1 + 1) * FFT_N2, :] = (hi * scale).astype(BF16)


def _fnet(fu, batch, seq):
    assert seq == FFT_N1 * FFT_N2
    assert fu.shape == (batch, FFT_N2, FFT_N1 * FNET_WIDTH)
    chan, pos = _dft_constants()
    twc, tws = _twiddle_tables(seq)
    scale = float((seq * FNET_GROUP_DIM) ** -0.5)
    vmem = 2 * FFT_N1 * FFT_N2 * FNET_WIDTH * 4 + 8 * seq * FNET_WIDTH * 2 + 16 * MIB
    return pl.pallas_call(
        functools.partial(_fnet_kernel, scale=scale),
        grid=(batch,),
        in_specs=[
            pl.BlockSpec((1, FFT_N2, FFT_N1 * FNET_WIDTH), lambda b: (b, 0, 0)),
            pl.BlockSpec(chan.shape, lambda b: (0, 0)),
            pl.BlockSpec(pos.shape, lambda b: (0, 0)),
            pl.BlockSpec(twc.shape, lambda b: (0, 0, 0)),
            pl.BlockSpec(tws.shape, lambda b: (0, 0, 0)),
        ],
        out_specs=pl.BlockSpec((seq, FNET_WIDTH), lambda b: (b, 0)),
        out_shape=jax.ShapeDtypeStruct((batch * seq, FNET_WIDTH), BF16),
        scratch_shapes=[pltpu.VMEM((FFT_N1 // 2 + 1, FFT_N2, FNET_WIDTH), F32),
                        pltpu.VMEM((FFT_N1 // 2, FFT_N2, FNET_WIDTH), F32)],
        compiler_params=_params(vmem, 1),
        name="fnet",
    )(fu, chan, pos, twc, tws)


def _ctx_mixer_kernel(sink_ref, qa_ref, ka_ref, va_ref, qb_ref, kb_ref, vb_ref, fu_ref, chan_ref,
                      pos_ref, oa_ref, ob_ref, oc_ref, *, n_ctx, scale):
    for blk in range(_NA_BLOCKS):
        sl = slice(blk * V7X_LANES, (blk + 1) * V7X_LANES)
        q = _stack_heads(qa_ref[:, sl])
        res = _attend(q, (ka_ref[:, sl],), (va_ref[:, blk * V_AUG:(blk + 1) * V_AUG],), (None,))
        oa_ref[:, sl] = jnp.where(_half_mask(n_ctx, 0), res[:n_ctx], res[n_ctx:]).astype(BF16)
    k = kb_ref[...]
    v = vb_ref[...]
    res = []
    for head in range(SWA_HEADS):
        sink = jnp.full((n_ctx, 1), sink_ref[head] * LOG2E, F32)
        res.append(_attend(qb_ref[head], (k,), (v,), (None,), extra_logit=sink))
    ob_ref[...] = _gather_swa_heads(res, n_ctx).astype(BF16)
    a_re, _ = _dft256_of_channels(fu_ref[...], chan_ref[...].astype(BF16), pos_ref[...].astype(BF16))
    oc_ref[...] = (a_re * scale).astype(BF16)


def _ctx_mixer(qa, ka, va, qb, kb, vb, fu, sink, batch, n_ctx):
    assert n_ctx == FFT_N2
    chan, pos = _dft_constants()
    scale = float((n_ctx * FNET_GROUP_DIM) ** -0.5)
    tok = lambda w: pl.BlockSpec((n_ctx, w), lambda b: (b, 0))
    widths = (NA_WIDTH, SWA_WIDTH, FNET_WIDTH)
    return pl.pallas_call(
        functools.partial(_ctx_mixer_kernel, n_ctx=n_ctx, scale=scale),
        grid=(batch,),
        in_specs=[
            pl.BlockSpec(memory_space=pltpu.SMEM),
            tok(NA_WIDTH), tok(NA_WIDTH), tok(_NA_BLOCKS * V_AUG),
            pl.BlockSpec((SWA_HEADS, n_ctx, V7X_LANES), lambda b: (0, b, 0)),
            tok(SWA_KV_WIDTH), tok(V_AUG), tok(FNET_WIDTH),
            pl.BlockSpec(chan.shape, lambda b: (0, 0)),
            pl.BlockSpec(pos.shape, lambda b: (0, 0)),
        ],
        out_specs=[tok(w) for w in widths],
        out_shape=[jax.ShapeDtypeStruct((batch * n_ctx, w), BF16) for w in widths],
        compiler_params=_params(24 * MIB, 1),
        name="ctx_mixer",
    )(sink, qa, ka, va, qb, kb, vb, fu, chan, pos)


def kernel(x, c, ctx, c_ctx, w_mod, b_mod, g_pre, g_post, w_ffn_in, w_ffn_out, w_in, w_out, na_rpb, swa_sink):
    batch, seq, _ = x.shape
    n_ctx = ctx.shape[1]
    depth = w_mod.shape[0]
    assert seq % (NA_SUBS_PER_STEP * NA_SUB_ROWS * GRID_W) == 0 and seq % TOKEN_TILE == 0
    assert seq % (SWA_BLOCKS_PER_STEP * SWA_BLOCK) == 0 and (batch * n_ctx) % TOKEN_TILE == 0

    mod_rows = -(-(batch + 1) // 8) * 8
    c_rows = jnp.concatenate(
        [c, c_ctx[None, :], jnp.zeros((mod_rows - batch - 1, D_MODEL), F32)], axis=0)
    mod_all = _modulation(c_rows, w_mod, b_mod).reshape(depth, mod_rows, N_MOD, D_MODEL)
    tiles_per_seq = seq // TOKEN_TILE
    latent_row = lambda i: i // tiles_per_seq
    ctx_row = lambda i: batch

    w1_all, w2_all = w_ffn_in.astype(BF16), w_ffn_out.astype(BF16)
    w_in_all, w_out_all = w_in.astype(BF16), w_out.astype(BF16)
    rope_tables = _rope_tables(seq)
    xl = x.reshape(batch * seq, D_MODEL)
    xc = ctx.reshape(batch * n_ctx, D_MODEL)

    for layer in range(depth):
        sink = swa_sink[layer].astype(F32)
        last = layer == depth - 1

        xl = _ffn(xl, mod_all, g_pre, g_post, w1_all, w2_all, layer, 0, latent_row)
        xc = _ffn(xc, mod_all, g_pre, g_post, w1_all, w2_all, layer, 0, ctx_row)

        qa, ka, va, qb, kb, vb, fu = _inproj(xl, mod_all, g_pre, w_in_all, layer, latent_row, seq, rope_tables)
        qac, kac, vac, qbc, kbc, vbc, fuc = _inproj(xc, mod_all, g_pre, w_in_all, layer, ctx_row, seq, None)

        bias_tab = _na_bias_tables(na_rpb[layer], seq // GRID_W)
        oa = _na_attention(qa, ka, va, kac, vac, bias_tab, batch, seq, n_ctx)
        ob = _swa_attention(qb, kb, vb, kbc, vbc, sink, batch, seq, n_ctx)
        oc = _fnet(fu, batch, seq)
        xl = _ffn(xl, mod_all, g_pre, g_post, w1_all, w2_all, layer, 1, latent_row,
                  mix=(oa, ob, oc, w_out_all))

        if not last:
            oac, obc, occ = _ctx_mixer(qac, kac, vac, qbc, kbc, vbc, fuc, sink, batch, n_ctx)
            xc = _ffn(xc, mod_all, g_pre, g_post, w1_all, w2_all, layer, 1, ctx_row,
                      mix=(oac, obc, occ, w_out_all))

    return xl.reshape(batch, seq, D_MODEL)
```

```python
import functools
import math

import numpy as np
import jax
import jax.numpy as jnp
from jax import lax
from jax.experimental import pallas as pl
from jax.experimental.pallas import tpu as pltpu

F32 = jnp.float32
BF16 = jnp.bfloat16

D_MODEL = 1024
GRID_W = 64
HEAD_DIM = 64
NA_HEADS = 6
NA_KR = 8
NA_KW = 16
SWA_HEADS = 6
SWA_KV_HEADS = 2
SWA_WINDOW = 128
SWA_BLOCK = 128
FNET_GROUPS = 4
FNET_GROUP_DIM = 64
NA_WIDTH = NA_HEADS * HEAD_DIM
SWA_WIDTH = SWA_HEADS * HEAD_DIM
SWA_KV_WIDTH = SWA_KV_HEADS * HEAD_DIM
FNET_WIDTH = FNET_GROUPS * FNET_GROUP_DIM
D_IN = 3 * NA_WIDTH + SWA_WIDTH + 2 * SWA_KV_WIDTH + FNET_WIDTH
D_FF = 256 * ((8 * D_MODEL // 3 + 255) // 256)
N_SUB = 3
N_MOD = 3 * N_SUB
MACARON_WEIGHT = 0.5
ROPE_BASE = 10000.0
RMS_EPS = 1e-6
NEG_INF = -1e30
LOG2E = math.log2(math.e)
QK_SCALE_LOG2 = HEAD_DIM ** -0.5 * LOG2E

V7X_LANES = 128
V7X_VMEM_LIMIT_BYTES = 60000 * 1024
MIB = 1024 * 1024

TOKEN_TILE = 1024
TOKEN_SUB = 512
FFN_CHUNK = 256
MOD_TILE = 1152
NA_SUB_ROWS = 4
NA_SUBS_PER_STEP = 16
SWA_BLOCKS_PER_STEP = 8
NA_KEY_ROWS = NA_SUB_ROWS + NA_KR
FFT_N1 = 16
FFT_N2 = 256

HEADS_PER_BLOCK = V7X_LANES // HEAD_DIM
V_AUG = 2 * V7X_LANES


def _params(vmem_bytes, n_grid):
    limit = int(min(max(vmem_bytes, 16 * MIB), V7X_VMEM_LIMIT_BYTES))
    return pltpu.CompilerParams(dimension_semantics=("arbitrary",) * n_grid, vmem_limit_bytes=limit)


def _rms(x, g):
    return x * lax.rsqrt(jnp.mean(x * x, axis=-1, keepdims=True) + RMS_EPS) * g


def _nt_dot(a, b):
    return lax.dot_general(a, b, (((1,), (1,)), ((), ())), preferred_element_type=F32)


def _dot(a, b):
    return jnp.dot(a, b, preferred_element_type=F32)


def _mod_kernel(c_ref, w_ref, b_ref, o_ref):
    a = c_ref[...]
    act = a / (1.0 + jnp.exp(-a))
    o_ref[0] = _dot(act.astype(BF16), w_ref[0].astype(BF16)) + b_ref[0]


def _modulation(c_rows, w_mod, b_mod):
    depth, _, n_out = w_mod.shape
    rows = c_rows.shape[0]
    return pl.pallas_call(
        _mod_kernel,
        grid=(depth, n_out // MOD_TILE),
        in_specs=[
            pl.BlockSpec((rows, D_MODEL), lambda l, j: (0, 0)),
            pl.BlockSpec((1, D_MODEL, MOD_TILE), lambda l, j: (l, 0, j)),
            pl.BlockSpec((1, 1, MOD_TILE), lambda l, j: (l, 0, j)),
        ],
        out_specs=pl.BlockSpec((1, rows, MOD_TILE), lambda l, j: (l, 0, j)),
        out_shape=jax.ShapeDtypeStruct((depth, rows, n_out), F32),
        compiler_params=_params(4 * D_MODEL * MOD_TILE * 4, 2),
        name="modulation",
    )(c_rows, w_mod, b_mod.reshape(depth, 1, n_out))


def _token_specs(layer, mod_row):
    tok = lambda w: pl.BlockSpec((TOKEN_TILE, w), lambda i: (i, 0))
    mod = pl.BlockSpec((None, 1, N_MOD, D_MODEL), lambda i: (layer, mod_row(i), 0, 0))
    gain = pl.BlockSpec((None, N_SUB, D_MODEL), lambda i: (layer, 0, 0))
    return tok, mod, gain


def _resident(shape, index):
    return pl.BlockSpec(shape, lambda i: index, pipeline_mode=pl.Buffered(1))


def _ffn_kernel(*refs, sub, mix):
    if mix:
        (x_ref, mod_ref, gpre_ref, gpost_ref, oa_ref, ob_ref, oc_ref, wo_ref, w1_ref, w2_ref,
         o_ref, act_ref) = refs
    else:
        x_ref, mod_ref, gpre_ref, gpost_ref, w1_ref, w2_ref, o_ref, act_ref = refs
    shift = mod_ref[0, 3 * sub:3 * sub + 1, :]
    scale = mod_ref[0, 3 * sub + 1:3 * sub + 2, :]
    gate = mod_ref[0, 3 * sub + 2:3 * sub + 3, :]
    parts = [slice(p * TOKEN_SUB, (p + 1) * TOKEN_SUB) for p in range(TOKEN_TILE // TOKEN_SUB)]
    xs = [x_ref[rows, :] for rows in parts]
    if mix:
        mixed = [_dot(jnp.concatenate([oa_ref[rows, :], ob_ref[rows, :], oc_ref[rows, :]], axis=1),
                      wo_ref[...]) for rows in parts]
        xs = [x + mod_ref[0, 5:6, :] * _rms(y, gpost_ref[1:2, :]) for x, y in zip(xs, mixed)]
    hbs = [(_rms(x, gpre_ref[sub:sub + 1, :]) * (1.0 + scale) + shift).astype(BF16) for x in xs]
    for part, (rows, x, hb) in enumerate(zip(parts, xs, hbs)):
        for c in range(D_FF // FFN_CHUNK):
            lo = c * FFN_CHUNK
            g = _dot(hb, w1_ref[:, lo:lo + FFN_CHUNK])
            u = _dot(hb, w1_ref[:, D_FF + lo:D_FF + lo + FFN_CHUNK])
            act_ref[part, :, lo:lo + FFN_CHUNK] = (g / (1.0 + jnp.exp(-g)) * u).astype(BF16)
        y = _dot(act_ref[part], w2_ref[...])
        o_ref[rows, :] = x + MACARON_WEIGHT * gate * _rms(y, gpost_ref[sub:sub + 1, :])


def _ffn(x, mod_all, g_pre, g_post, w1_all, w2_all, layer, which, mod_row, mix=None):
    n_tok = x.shape[0]
    sub = 2 * which
    tok, mod_spec, gain_spec = _token_specs(layer, mod_row)
    in_specs = [tok(D_MODEL), mod_spec, gain_spec, gain_spec]
    args = [x, mod_all, g_pre, g_post]
    vmem = ((D_MODEL * 2 * D_FF + D_FF * D_MODEL) * 2 + 4 * TOKEN_TILE * D_MODEL * 4
            + TOKEN_TILE * D_FF * 2 + 12 * MIB)
    if mix is not None:
        oa, ob, oc, w_out_all = mix
        in_specs += [tok(NA_WIDTH), tok(SWA_WIDTH), tok(FNET_WIDTH),
                     _resident((None, D_MODEL, D_MODEL), (layer, 0, 0))]
        args += [oa, ob, oc, w_out_all]
        vmem += D_MODEL * D_MODEL * 2 + 2 * TOKEN_TILE * D_MODEL * 2
    in_specs += [_resident((None, None, D_MODEL, 2 * D_FF), (layer, which, 0, 0)),
                 _resident((None, None, D_FF, D_MODEL), (layer, which, 0, 0))]
    args += [w1_all, w2_all]
    return pl.pallas_call(
        functools.partial(_ffn_kernel, sub=sub, mix=mix is not None),
        grid=(n_tok // TOKEN_TILE,),
        in_specs=in_specs,
        out_specs=tok(D_MODEL),
        out_shape=jax.ShapeDtypeStruct((n_tok, D_MODEL), F32),
        scratch_shapes=[pltpu.VMEM((TOKEN_TILE // TOKEN_SUB, TOKEN_SUB, D_FF), BF16)],
        compiler_params=_params(vmem, 1),
        name="mix_ffn" if mix is not None else "ffn",
    )(*args)


_NA_BLOCKS = NA_HEADS // HEADS_PER_BLOCK
_SWA_GROUP = SWA_HEADS // SWA_KV_HEADS
_INPROJ_GROUPS = (("qa", NA_WIDTH), ("ka", NA_WIDTH), ("va", NA_WIDTH), ("qb", SWA_WIDTH),
                  ("kb", SWA_KV_WIDTH), ("vb", SWA_KV_WIDTH), ("fu", FNET_WIDTH))
INPROJ_CHUNK = 2 * V7X_LANES


def _half_mask(rows, half):
    lane = lax.broadcasted_iota(jnp.int32, (rows, V7X_LANES), 1)
    return (lane >= HEAD_DIM) if half else (lane < HEAD_DIM)


def _inproj_kernel(*refs, latent):
    if latent:
        (x_ref, mod_ref, gpre_ref, w_ref, cos_ref, sina_ref, sinb_ref,
         qa_ref, ka_ref, va_ref, qb_ref, kb_ref, vb_ref, fu_ref, fu_scr) = refs
    else:
        (x_ref, mod_ref, gpre_ref, w_ref,
         qa_ref, ka_ref, va_ref, qb_ref, kb_ref, vb_ref, fu_ref) = refs
    shift = mod_ref[0, 3:4, :]
    scale = mod_ref[0, 4:5, :]
    ones = jnp.ones((TOKEN_SUB, V7X_LANES), BF16)
    lane = lambda j: slice(j * V7X_LANES, (j + 1) * V7X_LANES)
    rot = HEAD_DIM // 4

    def emit(kind, j, blk, rows, part):
        if kind in ("qb", "kb") and latent:
            blk = (blk * cos_ref[rows, :] + pltpu.roll(blk, V7X_LANES - rot, axis=1) * sina_ref[rows, :]
                   + pltpu.roll(blk, rot, axis=1) * sinb_ref[rows, :])
        if kind == "qa":
            qa_ref[rows, lane(j)] = (blk * QK_SCALE_LOG2).astype(BF16)
        elif kind == "ka":
            ka_ref[rows, lane(j)] = blk.astype(BF16)
        elif kind == "va":
            va_ref[rows, lane(2 * j)] = blk.astype(BF16)
            va_ref[rows, lane(2 * j + 1)] = ones
        elif kind == "qb":
            for half in range(HEADS_PER_BLOCK):
                head = j * HEADS_PER_BLOCK + half
                kv_half = head // _SWA_GROUP
                piece = blk if half == kv_half else pltpu.roll(blk, HEAD_DIM, axis=1)
                qb_ref[head, rows, :] = jnp.where(
                    _half_mask(TOKEN_SUB, kv_half), piece * QK_SCALE_LOG2, 0.0).astype(BF16)
        elif kind == "kb":
            kb_ref[rows, :] = blk.astype(BF16)
        elif kind == "vb":
            vb_ref[rows, lane(0)] = blk.astype(BF16)
            vb_ref[rows, lane(1)] = ones
        elif latent:
            fu_scr[part, j] = blk
        else:
            fu_ref[rows, lane(j)] = blk.astype(BF16)

    blocks = [(kind, j) for kind, width in _INPROJ_GROUPS for j in range(width // V7X_LANES)]
    per_dot = INPROJ_CHUNK // V7X_LANES
    for part in range(TOKEN_TILE // TOKEN_SUB):
        rows = slice(part * TOKEN_SUB, (part + 1) * TOKEN_SUB)
        h = _rms(x_ref[rows, :], gpre_ref[1:2, :]) * (1.0 + scale) + shift
        hb = h.astype(BF16)
        for c in range(D_IN // INPROJ_CHUNK):
            r = _dot(hb, w_ref[:, c * INPROJ_CHUNK:(c + 1) * INPROJ_CHUNK])
            for i in range(per_dot):
                kind, j = blocks[c * per_dot + i]
                emit(kind, j, r[:, lane(i)], rows, part)
        if latent:
            n2_rows = TOKEN_SUB // FFT_N1
            for n1 in range(FFT_N1):
                for blk in range(FNET_WIDTH // V7X_LANES):
                    lo = n1 * FNET_WIDTH + blk * V7X_LANES
                    fu_ref[0, part * n2_rows:(part + 1) * n2_rows, lo:lo + V7X_LANES] = (
                        fu_scr[part, blk, pl.ds(n1, n2_rows, stride=FFT_N1), :].astype(BF16))


def _rope_tables(seq):
    n_rows = seq // GRID_W
    half = HEAD_DIM // 4
    inv = ROPE_BASE ** (-jnp.arange(half, dtype=F32) / half)
    ang_r = jnp.arange(n_rows, dtype=F32)[:, None] * inv[None, :]
    ang_c = jnp.arange(GRID_W, dtype=F32)[:, None] * inv[None, :]
    by_row = lambda a: jnp.repeat(a, GRID_W, axis=0)
    by_col = lambda a: jnp.tile(a, (n_rows, 1))
    cos_h = jnp.concatenate([by_row(jnp.cos(ang_r))] * 2 + [by_col(jnp.cos(ang_c))] * 2, axis=1)
    sin_h = jnp.concatenate([by_row(jnp.sin(ang_r))] * 2 + [by_col(jnp.sin(ang_c))] * 2, axis=1)
    first = (np.arange(HEAD_DIM) % (2 * half)) < half
    sina_h = jnp.where(first[None, :], -sin_h, 0.0)
    sinb_h = jnp.where(first[None, :], 0.0, sin_h)
    two = lambda a: jnp.concatenate([a, a], axis=1)
    return two(cos_h), two(sina_h), two(sinb_h)


def _inproj(x, mod_all, g_pre, w_in_all, layer, mod_row, seq, rope_tables):
    n_tok = x.shape[0]
    latent = rope_tables is not None
    tok_spec, mod_spec, gain_spec = _token_specs(layer, mod_row)
    in_specs = [tok_spec(D_MODEL), mod_spec, gain_spec, _resident((None, D_MODEL, D_IN), (layer, 0, 0))]
    args = [x, mod_all, g_pre, w_in_all]
    tok = lambda w: (tok_spec(w), jax.ShapeDtypeStruct((n_tok, w), BF16))
    outs = [tok(NA_WIDTH), tok(NA_WIDTH), tok(_NA_BLOCKS * V_AUG),
            (pl.BlockSpec((SWA_HEADS, TOKEN_TILE, V7X_LANES), lambda i: (0, i, 0)),
             jax.ShapeDtypeStruct((SWA_HEADS, n_tok, V7X_LANES), BF16)),
            tok(SWA_KV_WIDTH), tok(V_AUG)]
    scratch = []
    if latent:
        tiles_per_seq = seq // TOKEN_TILE
        for tab in rope_tables:
            in_specs.append(pl.BlockSpec((TOKEN_TILE, V7X_LANES), lambda i: (i % tiles_per_seq, 0)))
            args.append(tab)
        rows = TOKEN_TILE // FFT_N1
        outs.append((pl.BlockSpec((1, rows, FFT_N1 * FNET_WIDTH),
                                  lambda i: (i // tiles_per_seq, i % tiles_per_seq, 0)),
                     jax.ShapeDtypeStruct((n_tok // seq, seq // FFT_N1, FFT_N1 * FNET_WIDTH), BF16)))
        scratch.append(pltpu.VMEM((TOKEN_TILE // TOKEN_SUB, FNET_WIDTH // V7X_LANES, TOKEN_SUB, V7X_LANES), F32))
    else:
        outs.append(tok(FNET_WIDTH))
    vmem = D_MODEL * D_IN * 2 + 4 * TOKEN_TILE * D_MODEL * 4 + 4 * TOKEN_TILE * D_IN * 4 + 8 * MIB
    return pl.pallas_call(
        functools.partial(_inproj_kernel, latent=latent),
        grid=(n_tok // TOKEN_TILE,),
        in_specs=in_specs,
        out_specs=[s for s, _ in outs],
        out_shape=[s for _, s in outs],
        scratch_shapes=scratch,
        compiler_params=_params(vmem, 1),
        name="inproj_latent" if latent else "inproj_ctx",
    )(*args)


def _stack_heads(q):
    rows = q.shape[0]
    zero = jnp.zeros_like(q)
    return jnp.concatenate([jnp.where(_half_mask(rows, 0), q, zero),
                            jnp.where(_half_mask(rows, 1), q, zero)], axis=0)


def _attend(q, keys, values, biases, extra_logit=None):
    scores = []
    for k, b in zip(keys, biases):
        s = _nt_dot(q, k)
        scores.append(s if b is None else s + b)
    m = scores[0].max(axis=-1, keepdims=True)
    for s in scores[1:]:
        m = jnp.maximum(m, s.max(axis=-1, keepdims=True))
    if extra_logit is not None:
        m = jnp.maximum(m, extra_logit)
    acc = None
    for s, v in zip(scores, values):
        pv = _dot(jnp.exp2(s - m).astype(BF16), v)
        acc = pv if acc is None else acc + pv
    num = acc[:, :V7X_LANES]
    den = acc[:, V7X_LANES:]
    if extra_logit is not None:
        den = den + jnp.exp2(extra_logit - m)
    return num / den


_NA_Q = NA_SUB_ROWS * GRID_W
_NA_K = NA_KEY_ROWS * GRID_W
_NA_M = HEADS_PER_BLOCK * _NA_Q


def _na_kernel(q_ref, k_ref, v_ref, kc_ref, vc_ref, bias_ref, o_ref, *, grid_rows):
    j = pl.program_id(2)
    n_slabs = grid_rows // NA_SUB_ROWS
    kc = kc_ref[...]
    vc = vc_ref[...]
    for sub in range(NA_SUBS_PER_STEP):
        slab = j * NA_SUBS_PER_STEP + sub
        case = jnp.where(slab == 0, 0, jnp.where(slab == n_slabs - 1, 2, 1))
        start_row = jnp.clip(slab * NA_SUB_ROWS - NA_KR // 2, 0, grid_rows - NA_KEY_ROWS)
        start = pl.multiple_of(start_row * GRID_W, NA_SUB_ROWS * GRID_W)
        q2 = q_ref[sub * _NA_Q:(sub + 1) * _NA_Q, :]
        k = k_ref[pl.ds(start, _NA_K), :]
        v = v_ref[pl.ds(start, _NA_K), :]
        res = []
        for half in range(HEADS_PER_BLOCK):
            q = jnp.where(_half_mask(_NA_Q, half), q2, jnp.zeros_like(q2))
            bias = bias_ref[case, 0, half * _NA_Q:(half + 1) * _NA_Q, :]
            res.append(_attend(q, (k, kc), (v, vc), (bias, None)))
        o_ref[sub * _NA_Q:(sub + 1) * _NA_Q, :] = jnp.where(
            _half_mask(_NA_Q, 0), res[0], res[1]).astype(BF16)


def _na_window_tables(grid_rows):
    n_subs = grid_rows // NA_SUB_ROWS
    kr = min(NA_KR, grid_rows)
    ro = np.zeros((3, NA_SUB_ROWS, NA_KEY_ROWS), np.int32)
    rvalid = np.zeros(ro.shape, bool)
    for ci, u in enumerate((0, min(1, n_subs - 1), n_subs - 1)):
        start_row = int(np.clip(u * NA_SUB_ROWS - NA_KR // 2, 0, grid_rows - NA_KEY_ROWS))
        for l in range(NA_SUB_ROWS):
            rq = u * NA_SUB_ROWS + l
            rs = int(np.clip(rq - kr // 2, 0, grid_rows - kr))
            for i in range(NA_KEY_ROWS):
                rk = start_row + i
                rvalid[ci, l, i] = rs <= rk < rs + kr
                ro[ci, l, i] = int(np.clip(rk - rq + NA_KR - 1, 0, 2 * NA_KR - 2))
    return ro, rvalid


def _na_bias_kernel(rpb_ref, o_ref, toep_ref, *, ro, rvalid, n_dr, n_dc):
    p = pl.program_id(0)
    cq = lax.broadcasted_iota(jnp.int32, (GRID_W, GRID_W), 0)
    ck = lax.broadcasted_iota(jnp.int32, (GRID_W, GRID_W), 1)
    ws = jnp.clip(cq - NA_KW // 2, 0, GRID_W - NA_KW)
    cvalid = (ck >= ws) & (ck < ws + NA_KW)
    co = ck - cq + NA_KW - 1
    neg = jnp.full((GRID_W, GRID_W), NEG_INF, F32)
    n_cases, n_l, n_i = ro.shape
    for hh in range(HEADS_PER_BLOCK):
        head = p * HEADS_PER_BLOCK + hh
        for dr in range(n_dr):
            tile = neg
            for dc in range(n_dc):
                tile = jnp.where(co == dc, rpb_ref[(head * n_dr + dr) * n_dc + dc] * LOG2E, tile)
            toep_ref[dr] = jnp.where(cvalid, tile, neg)
        for c in range(n_cases):
            for l in range(n_l):
                r0 = (hh * n_l + l) * GRID_W
                for i in range(n_i):
                    tile = toep_ref[int(ro[c, l, i])] if rvalid[c, l, i] else neg
                    o_ref[c, 0, r0:r0 + GRID_W, i * GRID_W:(i + 1) * GRID_W] = tile


def _na_bias_tables(rpb, grid_rows):
    ro, rvalid = _na_window_tables(grid_rows)
    n_dr, n_dc = rpb.shape[1], rpb.shape[2]
    return pl.pallas_call(
        functools.partial(_na_bias_kernel, ro=ro, rvalid=rvalid, n_dr=n_dr, n_dc=n_dc),
        grid=(_NA_BLOCKS,),
        in_specs=[pl.BlockSpec(memory_space=pltpu.SMEM)],
        out_specs=pl.BlockSpec((3, 1, _NA_M, _NA_K), lambda p: (0, p, 0, 0)),
        out_shape=jax.ShapeDtypeStruct((3, _NA_BLOCKS, _NA_M, _NA_K), F32),
        scratch_shapes=[pltpu.VMEM((n_dr, GRID_W, GRID_W), F32)],
        compiler_params=_params(4 * 3 * _NA_M * _NA_K * 4, 1),
        name="na_bias",
    )(rpb.astype(F32).reshape(-1))


def _na_attention(qa, ka, va, kac, vac, bias_tab, batch, seq, n_ctx):
    grid_rows = seq // GRID_W
    rows_q = NA_SUBS_PER_STEP * _NA_Q
    n_steps = seq // rows_q
    bias_block = (3, 1, _NA_M, _NA_K)
    vmem = (2 * 3 * _NA_M * _NA_K * 4 + 2 * seq * (V7X_LANES + V_AUG) * 2 + 16 * MIB)
    return pl.pallas_call(
        functools.partial(_na_kernel, grid_rows=grid_rows),
        grid=(batch, _NA_BLOCKS, n_steps),
        in_specs=[
            pl.BlockSpec((rows_q, V7X_LANES), lambda b, p, j: (b * n_steps + j, p)),
            pl.BlockSpec((seq, V7X_LANES), lambda b, p, j: (b, p)),
            pl.BlockSpec((seq, V_AUG), lambda b, p, j: (b, p)),
            pl.BlockSpec((n_ctx, V7X_LANES), lambda b, p, j: (b, p)),
            pl.BlockSpec((n_ctx, V_AUG), lambda b, p, j: (b, p)),
            pl.BlockSpec(bias_block, lambda b, p, j: (0, p, 0, 0)),
        ],
        out_specs=pl.BlockSpec((rows_q, V7X_LANES), lambda b, p, j: (b * n_steps + j, p)),
        out_shape=jax.ShapeDtypeStruct((batch * seq, NA_WIDTH), BF16),
        compiler_params=_params(vmem, 3),
        name="na_attention",
    )(qa, ka, va, kac, vac, bias_tab)


_SWA_SPAN = 3 * SWA_BLOCK


def _gather_swa_heads(res, rows):
    blocks = []
    for blk in range(SWA_HEADS // HEADS_PER_BLOCK):
        halves = []
        for half in range(HEADS_PER_BLOCK):
            head = blk * HEADS_PER_BLOCK + half
            piece = res[head]
            if head // _SWA_GROUP != half:
                piece = pltpu.roll(piece, HEAD_DIM, axis=1)
            halves.append(piece)
        blocks.append(jnp.where(_half_mask(rows, 0), halves[0], halves[1]))
    return jnp.concatenate(blocks, axis=1)


def _swa_kernel(sink_ref, q_ref, k_ref, v_ref, kc_ref, vc_ref, mask_ref, o_ref, *, seq):
    n = pl.program_id(1)
    nb = seq // SWA_BLOCK
    kc = kc_ref[...]
    vc = vc_ref[...]
    for sub in range(SWA_BLOCKS_PER_STEP):
        blk = n * SWA_BLOCKS_PER_STEP + sub
        case = jnp.where(blk == 0, 0, jnp.where(blk == nb - 1, 2, 1))
        start = pl.multiple_of(jnp.clip((blk - 1) * SWA_BLOCK, 0, seq - _SWA_SPAN), SWA_BLOCK)
        k = k_ref[pl.ds(start, _SWA_SPAN), :]
        v = v_ref[pl.ds(start, _SWA_SPAN), :]
        mask = mask_ref[case]
        rows = slice(sub * SWA_BLOCK, (sub + 1) * SWA_BLOCK)
        res = []
        for head in range(SWA_HEADS):
            sink = jnp.full((SWA_BLOCK, 1), sink_ref[head] * LOG2E, F32)
            res.append(_attend(q_ref[head, rows, :], (k, kc), (v, vc), (mask, None), extra_logit=sink))
        o_ref[rows, :] = _gather_swa_heads(res, SWA_BLOCK).astype(BF16)


def _swa_masks(seq):
    nb = seq // SWA_BLOCK
    out = np.zeros((3, SWA_BLOCK, _SWA_SPAN), np.float32)
    for ci, n in enumerate((0, min(1, nb - 1), nb - 1)):
        start = int(np.clip((n - 1) * SWA_BLOCK, 0, seq - _SWA_SPAN))
        kpos = start + np.arange(_SWA_SPAN)[None, :]
        qpos = n * SWA_BLOCK + np.arange(SWA_BLOCK)[:, None]
        out[ci] = np.where(np.abs(kpos - qpos) <= SWA_WINDOW, 0.0, NEG_INF)
    return jnp.asarray(out)


def _swa_attention(qb, kb, vb, kbc, vbc, sink, batch, seq, n_ctx):
    rows = SWA_BLOCKS_PER_STEP * SWA_BLOCK
    ns = seq // rows
    vmem = 4 * seq * (SWA_KV_WIDTH + V_AUG) * 2 + 16 * MIB
    return pl.pallas_call(
        functools.partial(_swa_kernel, seq=seq),
        grid=(batch, ns),
        in_specs=[
            pl.BlockSpec(memory_space=pltpu.SMEM),
            pl.BlockSpec((SWA_HEADS, rows, V7X_LANES), lambda b, n: (0, b * ns + n, 0)),
            pl.BlockSpec((seq, SWA_KV_WIDTH), lambda b, n: (b, 0)),
            pl.BlockSpec((seq, V_AUG), lambda b, n: (b, 0)),
            pl.BlockSpec((n_ctx, SWA_KV_WIDTH), lambda b, n: (b, 0)),
            pl.BlockSpec((n_ctx, V_AUG), lambda b, n: (b, 0)),
            pl.BlockSpec((3, SWA_BLOCK, _SWA_SPAN), lambda b, n: (0, 0, 0)),
        ],
        out_specs=pl.BlockSpec((rows, SWA_WIDTH), lambda b, n: (b * ns + n, 0)),
        out_shape=jax.ShapeDtypeStruct((batch * seq, SWA_WIDTH), BF16),
        compiler_params=_params(vmem, 2),
        name="swa_attention",
    )(sink, qb, kb, vb, kbc, vbc, _swa_masks(seq))


def _dft_constants():
    jc = np.arange(FNET_GROUP_DIM)
    ang_c = 2.0 * np.pi * ((jc[:, None] * jc[None, :]) % FNET_GROUP_DIM) / FNET_GROUP_DIM
    eye = np.eye(FNET_GROUPS)
    bd_cos = np.kron(eye, np.cos(ang_c))
    bd_sin = np.kron(eye, np.sin(ang_c))
    chan = np.concatenate([bd_cos, -bd_sin], axis=1)
    jp = np.arange(FFT_N2)
    ang_p = 2.0 * np.pi * ((jp[:, None] * jp[None, :]) % FFT_N2) / FFT_N2
    cp, sp = np.cos(ang_p), np.sin(ang_p)
    pos = np.block([[cp, sp], [-sp, cp]])
    return jnp.asarray(chan, F32), jnp.asarray(pos, F32)


def _twiddle_tables(n_total):
    n1 = np.arange(FFT_N1)[:, None]
    k2 = np.arange(FFT_N2)[None, :]
    ang = 2.0 * np.pi * ((n1 * k2) % n_total) / n_total
    bc = lambda a: jnp.asarray(np.repeat(a[:, :, None], V7X_LANES, axis=2), F32)
    return bc(np.cos(ang)), bc(np.sin(ang))


def _dft256_of_channels(u, chan, pos):
    z = _dot(u, chan)
    zs = jnp.concatenate([z[:, :FNET_WIDTH], z[:, FNET_WIDTH:]], axis=0).astype(BF16)
    a = _dot(pos, zs)
    return a[:FFT_N2], a[FFT_N2:]


def _weighted_sum(ref, coeffs):
    acc = None
    for n, w in enumerate(coeffs):
        if abs(w) < 1e-9:
            continue
        term = ref[n] if abs(abs(w) - 1.0) < 1e-9 else abs(w) * ref[n]
        if acc is None:
            acc = term if w > 0 else -term
        else:
            acc = acc + term if w > 0 else acc - term
    return acc


def _fnet_kernel(u_ref, chan_ref, pos_ref, twc_ref, tws_ref, o_ref, sre_ref, dim_ref, *, scale):
    chan = chan_ref[...].astype(BF16)
    pos = pos_ref[...].astype(BF16)
    half = FFT_N1 // 2

    def twiddled(n1):
        u = u_ref[0, :, n1 * FNET_WIDTH:(n1 + 1) * FNET_WIDTH]
        a_re, a_im = _dft256_of_channels(u, chan, pos)
        if n1 == 0:
            return a_re, a_im
        c = jnp.concatenate([twc_ref[n1]] * (FNET_WIDTH // V7X_LANES), axis=1)
        s = jnp.concatenate([tws_ref[n1]] * (FNET_WIDTH // V7X_LANES), axis=1)
        return a_re * c + a_im * s, a_im * c - a_re * s

    sre_ref[0] = twiddled(0)[0]
    sre_ref[half] = twiddled(half)[0]
    for n1 in range(1, half):
        lo_re, lo_im = twiddled(n1)
        hi_re, hi_im = twiddled(FFT_N1 - n1)
        sre_ref[n1] = lo_re + hi_re
        dim_ref[n1] = lo_im - hi_im
    for k1 in range(half + 1):
        ang = [2.0 * np.pi * ((n1 * k1) % FFT_N1) / FFT_N1 for n1 in range(half + 1)]
        p = _weighted_sum(sre_ref, [float(np.cos(a)) for a in ang])
        q = _weighted_sum(dim_ref, [0.0] + [float(np.sin(a)) for a in ang[1:half]])
        lo = p if q is None else p + q
        o_ref[k1 * FFT_N2:(k1 + 1) * FFT_N2, :] = (lo * scale).astype(BF16)
        if 0 < k1 < half:
            hi = p - q
            o_ref[(FFT_N1 - k1) * FFT_N2:(FFT_N1 - k1 + 1) * FFT_N2, :] = (hi * scale).astype(BF16)


def _fnet(fu, batch, seq):
    assert seq == FFT_N1 * FFT_N2
    assert fu.shape == (batch, FFT_N2, FFT_N1 * FNET_WIDTH)
    chan, pos = _dft_constants()
    twc, tws = _twiddle_tables(seq)
    scale = float((seq * FNET_GROUP_DIM) ** -0.5)
    vmem = 2 * FFT_N1 * FFT_N2 * FNET_WIDTH * 4 + 8 * seq * FNET_WIDTH * 2 + 16 * MIB
    return pl.pallas_call(
        functools.partial(_fnet_kernel, scale=scale),
        grid=(batch,),
        in_specs=[
            pl.BlockSpec((1, FFT_N2, FFT_N1 * FNET_WIDTH), lambda b: (b, 0, 0)),
            pl.BlockSpec(chan.shape, lambda b: (0, 0)),
            pl.BlockSpec(pos.shape, lambda b: (0, 0)),
            pl.BlockSpec(twc.shape, lambda b: (0, 0, 0)),
            pl.BlockSpec(tws.shape, lambda b: (0, 0, 0)),
        ],
        out_specs=pl.BlockSpec((seq, FNET_WIDTH), lambda b: (b, 0)),
        out_shape=jax.ShapeDtypeStruct((batch * seq, FNET_WIDTH), BF16),
        scratch_shapes=[pltpu.VMEM((FFT_N1 // 2 + 1, FFT_N2, FNET_WIDTH), F32),
                        pltpu.VMEM((FFT_N1 // 2, FFT_N2, FNET_WIDTH), F32)],
        compiler_params=_params(vmem, 1),
        name="fnet",
    )(fu, chan, pos, twc, tws)


def _ctx_mixer_kernel(sink_ref, qa_ref, ka_ref, va_ref, qb_ref, kb_ref, vb_ref, fu_ref, chan_ref,
                      pos_ref, oa_ref, ob_ref, oc_ref, *, n_ctx, scale):
    for blk in range(_NA_BLOCKS):
        sl = slice(blk * V7X_LANES, (blk + 1) * V7X_LANES)
        q = _stack_heads(qa_ref[:, sl])
        res = _attend(q, (ka_ref[:, sl],), (va_ref[:, blk * V_AUG:(blk + 1) * V_AUG],), (None,))
        oa_ref[:, sl] = jnp.where(_half_mask(n_ctx, 0), res[:n_ctx], res[n_ctx:]).astype(BF16)
    k = kb_ref[...]
    v = vb_ref[...]
    res = []
    for head in range(SWA_HEADS):
        sink = jnp.full((n_ctx, 1), sink_ref[head] * LOG2E, F32)
        res.append(_attend(qb_ref[head], (k,), (v,), (None,), extra_logit=sink))
    ob_ref[...] = _gather_swa_heads(res, n_ctx).astype(BF16)
    a_re, _ = _dft256_of_channels(fu_ref[...], chan_ref[...].astype(BF16), pos_ref[...].astype(BF16))
    oc_ref[...] = (a_re * scale).astype(BF16)


def _ctx_mixer(qa, ka, va, qb, kb, vb, fu, sink, batch, n_ctx):
    assert n_ctx == FFT_N2
    chan, pos = _dft_constants()
    scale = float((n_ctx * FNET_GROUP_DIM) ** -0.5)
    tok = lambda w: pl.BlockSpec((n_ctx, w), lambda b: (b, 0))
    widths = (NA_WIDTH, SWA_WIDTH, FNET_WIDTH)
    return pl.pallas_call(
        functools.partial(_ctx_mixer_kernel, n_ctx=n_ctx, scale=scale),
        grid=(batch,),
        in_specs=[
            pl.BlockSpec(memory_space=pltpu.SMEM),
            tok(NA_WIDTH), tok(NA_WIDTH), tok(_NA_BLOCKS * V_AUG),
            pl.BlockSpec((SWA_HEADS, n_ctx, V7X_LANES), lambda b: (0, b, 0)),
            tok(SWA_KV_WIDTH), tok(V_AUG), tok(FNET_WIDTH),
            pl.BlockSpec(chan.shape, lambda b: (0, 0)),
            pl.BlockSpec(pos.shape, lambda b: (0, 0)),
        ],
        out_specs=[tok(w) for w in widths],
        out_shape=[jax.ShapeDtypeStruct((batch * n_ctx, w), BF16) for w in widths],
        compiler_params=_params(24 * MIB, 1),
        name="ctx_mixer",
    )(sink, qa, ka, va, qb, kb, vb, fu, chan, pos)


def kernel(x, c, ctx, c_ctx, w_mod, b_mod, g_pre, g_post, w_ffn_in, w_ffn_out, w_in, w_out, na_rpb, swa_sink):
    batch, seq, _ = x.shape
    n_ctx = ctx.shape[1]
    depth = w_mod.shape[0]
    assert seq % (NA_SUBS_PER_STEP * NA_SUB_ROWS * GRID_W) == 0 and seq % TOKEN_TILE == 0
    assert seq % (SWA_BLOCKS_PER_STEP * SWA_BLOCK) == 0 and (batch * n_ctx) % TOKEN_TILE == 0

    mod_rows = -(-(batch + 1) // 8) * 8
    c_rows = jnp.concatenate(
        [c, c_ctx[None, :], jnp.zeros((mod_rows - batch - 1, D_MODEL), F32)], axis=0)
    mod_all = _modulation(c_rows, w_mod, b_mod).reshape(depth, mod_rows, N_MOD, D_MODEL)
    tiles_per_seq = seq // TOKEN_TILE
    latent_row = lambda i: i // tiles_per_seq
    ctx_row = lambda i: batch

    w1_all, w2_all = w_ffn_in.astype(BF16), w_ffn_out.astype(BF16)
    w_in_all, w_out_all = w_in.astype(BF16), w_out.astype(BF16)
    rope_tables = _rope_tables(seq)
    xl = x.reshape(batch * seq, D_MODEL)
    xc = ctx.reshape(batch * n_ctx, D_MODEL)

    for layer in range(depth):
        sink = swa_sink[layer].astype(F32)
        last = layer == depth - 1

        xl = _ffn(xl, mod_all, g_pre, g_post, w1_all, w2_all, layer, 0, latent_row)
        xc = _ffn(xc, mod_all, g_pre, g_post, w1_all, w2_all, layer, 0, ctx_row)

        qa, ka, va, qb, kb, vb, fu = _inproj(xl, mod_all, g_pre, w_in_all, layer, latent_row, seq, rope_tables)
        qac, kac, vac, qbc, kbc, vbc, fuc = _inproj(xc, mod_all, g_pre, w_in_all, layer, ctx_row, seq, None)

        bias_tab = _na_bias_tables(na_rpb[layer], seq // GRID_W)
        oa = _na_attention(qa, ka, va, kac, vac, bias_tab, batch, seq, n_ctx)
        ob = _swa_attention(qb, kb, vb, kbc, vbc, sink, batch, seq, n_ctx)
        oc = _fnet(fu, batch, seq)
        xl = _ffn(xl, mod_all, g_pre, g_post, w1_all, w2_all, layer, 1, latent_row,
                  mix=(oa, ob, oc, w_out_all))

        if not last:
            oac, obc, occ = _ctx_mixer(qac, kac, vac, qbc, kbc, vbc, fuc, sink, batch, n_ctx)
            xc = _ffn(xc, mod_all, g_pre, g_post, w1_all, w2_all, layer, 1, ctx_row,
                      mix=(oac, obc, occ, w_out_all))

    return xl.reshape(batch, seq, D_MODEL)
```
